```python
import jax, jax.numpy as jnp
from jax import lax
import numpy as np

D_MODEL = 2048
BATCH = 2
SEQ = 4096
DEPTH = 4
DEC_BATCH = 8
DEC_SEQ = 8
PAST_LEN = 16384
PAGE_SIZE = 128

N_META = 16
D_FF = 4 * D_MODEL
N_EVEN = (DEPTH + 1) // 2
N_ODD = DEPTH // 2
D_CONV = D_MODEL // 2
CONV_W = 3
D_MLSTM = D_MODEL // 2
MLSTM_HEADS = 4
HD_B = D_MLSTM // MLSTM_HEADS
MLSTM_CHUNK = 64
MLSTM_F_BIAS = 3.0
FOX_HEADS = 16
FOX_HEAD_DIM = D_MODEL // FOX_HEADS
D_FOX = FOX_HEADS * FOX_HEAD_DIM
FOX_F_BIAS = 4.0
BLOCK_Q = 128
D_IN_AB = 3 * D_CONV + 4 * D_MLSTM + 2 * MLSTM_HEADS
D_IN_C = 3 * D_FOX + FOX_HEADS
EPS = 1e-6
M_INIT = -1e30

kernel_name = 'hybrid_conv_mlstm_fox_decode_step'


def rmsnorm(x, g):
    xf = x.astype(jnp.float32)
    y = xf * lax.rsqrt(jnp.mean(xf * xf, axis=-1, keepdims=True) + EPS)
    return (y * g.astype(jnp.float32)).astype(x.dtype)


def sq_relu_mlp(x, w_up, w_down):
    h = jax.nn.relu(x @ w_up)
    return (h * h) @ w_down


def causal_dwconv(u, prev, w):
    ext = jnp.concatenate([prev, u], axis=1)
    L = u.shape[1]
    y = sum(w[j] * ext[:, j:j + L] for j in range(CONV_W))
    return y, ext[:, L:]


def mlstm_chunk(state, xs):
    C, n, m = state
    q, k, v, li, lf = xs
    L = q.shape[2]
    b = jnp.cumsum(lf, axis=-1)
    causal = jnp.tril(jnp.ones((L, L), dtype=bool))
    D = jnp.where(causal, b[..., :, None] - b[..., None, :] + li[..., None, :], -jnp.inf)
    g = m[..., None] + b
    m_t = jnp.maximum(g, jnp.max(D, axis=-1))
    S = jnp.einsum('bhtk,bhsk->bhts', q, k) * jnp.exp(D - m_t[..., None])
    inter = jnp.exp(g - m_t)
    num = jnp.einsum('bhts,bhsv->bhtv', S, v) + inter[..., None] * jnp.einsum('bhvk,bhtk->bhtv', C, q)
    den = jnp.sum(S, axis=-1) + inter * jnp.einsum('bhk,bhtk->bht', n, q)
    h = num / jnp.maximum(jnp.abs(den), jnp.exp(-m_t))[..., None]
    m_new = m_t[..., -1]
    w_s = jnp.exp(b[..., -1:] - b + li - m_new[..., None])
    decay = jnp.exp(g[..., -1] - m_new)
    C_new = decay[..., None, None] * C + jnp.einsum('bhs,bhsv,bhsk->bhvk', w_s, v, k)
    n_new = decay[..., None] * n + jnp.einsum('bhs,bhsk->bhk', w_s, k)
    return (C_new, n_new, m_new), h


def mlstm_seq(state, q, k, v, li, lf, lead):
    state, h0 = mlstm_chunk(state, (q[:, :, :lead], k[:, :, :lead], v[:, :, :lead], li[..., :lead], lf[..., :lead]))
    hs = [h0]
    rest = q.shape[2] - lead
    if rest > 0:
        nc = rest // MLSTM_CHUNK

        def to_chunks(a):
            a = a[:, :, lead:]
            a = a.reshape(a.shape[:2] + (nc, MLSTM_CHUNK) + a.shape[3:])
            return jnp.moveaxis(a, 2, 0)

        state, hc = lax.scan(mlstm_chunk, state, tuple(to_chunks(a) for a in (q, k, v, li, lf)))
        hc = jnp.moveaxis(hc, 0, 2)
        hs.append(hc.reshape(hc.shape[:2] + (rest, hc.shape[-1])))
    return state, jnp.concatenate(hs, axis=2)


def mixer_ab(xn, conv_prev, mstate, w_in, conv_w, b_i, b_f, norm_h, w_out, lead):
    B, L, _ = xn.shape
    f32 = jnp.float32
    sizes = [D_CONV] * 3 + [D_MLSTM] * 4 + [MLSTM_HEADS] * 2
    idx = [int(s) for s in np.cumsum(sizes)[:-1]]
    a_b, a_c, a_x, q, k, v, o, ig, fg = jnp.split(xn @ w_in, idx, axis=-1)
    z, conv_new = causal_dwconv(a_c * a_x, conv_prev, conv_w)
    y_a = a_b * z
    heads = lambda t: t.reshape(B, L, MLSTM_HEADS, HD_B).transpose(0, 2, 1, 3).astype(f32)
    qh, kh, vh = heads(q), heads(k) * (HD_B ** -0.5), heads(v)
    li = (ig + b_i).astype(f32).transpose(0, 2, 1)
    lf = jax.nn.log_sigmoid((fg + b_f).astype(f32)).transpose(0, 2, 1)
    mstate, h = mlstm_seq(mstate, qh, kh, vh, li, lf, lead)
    mu = jnp.mean(h, axis=-1, keepdims=True)
    var = jnp.mean((h - mu) ** 2, axis=-1, keepdims=True)
    h = ((h - mu) * lax.rsqrt(var + EPS)).transpose(0, 2, 1, 3).reshape(B, L, D_MLSTM)
    y_b = jax.nn.sigmoid(o) * (h * norm_h.astype(f32)).astype(xn.dtype)
    out = jnp.concatenate([y_a, y_b], axis=-1) @ w_out
    return out, conv_new, mstate


def fox_project(xn, w_in, b_f):
    B, L, _ = xn.shape
    q, k, v, fg = jnp.split(xn @ w_in, [D_FOX, 2 * D_FOX, 3 * D_FOX], axis=-1)
    sh = (B, L, FOX_HEADS, FOX_HEAD_DIM)
    lf = jax.nn.log_sigmoid((fg + b_f).astype(jnp.float32))
    return q.reshape(sh), k.reshape(sh), v.reshape(sh), lf


def fox_prompt_attn(q, k, v, lf):
    B, T, H, d = q.shape
    nb = -(-T // BLOCK_Q)
    Tp = nb * BLOCK_Q
    pad = Tp - T
    c = jnp.cumsum(lf, axis=1)
    padf = lambda a: jnp.pad(a, [(0, 0), (0, pad)] + [(0, 0)] * (a.ndim - 2))
    qp, kp, vp, cp = padf(q), padf(k), padf(v), padf(c)
    c_keys = cp.transpose(0, 2, 1)
    key_pos = jnp.arange(Tp)
    scale = d ** -0.5

    def block(i):
        start = i * BLOCK_Q
        qb = lax.dynamic_slice_in_dim(qp, start, BLOCK_Q, axis=1)
        cb = lax.dynamic_slice_in_dim(cp, start, BLOCK_Q, axis=1).transpose(0, 2, 1)
        s = jnp.einsum('bqhd,bkhd->bhqk', qb, kp).astype(jnp.float32) * scale
        s = s + cb[..., :, None] - c_keys[:, :, None, :]
        q_pos = start + jnp.arange(BLOCK_Q)
        s = jnp.where(key_pos[None, :] <= q_pos[:, None], s, -jnp.inf)
        p = jax.nn.softmax(s, axis=-1).astype(vp.dtype)
        return jnp.einsum('bhqk,bkhd->bqhd', p, vp)

    o = lax.map(block, jnp.arange(nb))
    return jnp.moveaxis(o, 0, 1).reshape(B, Tp, H, d)[:, :T]


def fox_sample_attn(q, k, v, lf, k_past, v_past, lf_past):
    S, d = q.shape[1], q.shape[-1]
    P = k_past.shape[1]
    scale = d ** -0.5
    c_new = jnp.cumsum(lf, axis=1).transpose(0, 2, 1)
    lfp = lf_past.astype(jnp.float32)
    rev = (lax.cumsum(lfp, axis=1, reverse=True) - lfp).transpose(0, 2, 1)
    s_past = jnp.einsum('bqhd,bkhd->bhqk', q, k_past).astype(jnp.float32) * scale + c_new[..., :, None] + rev[..., None, :]
    s_new = jnp.einsum('bqhd,bkhd->bhqk', q, k).astype(jnp.float32) * scale + c_new[..., :, None] - c_new[..., None, :]
    s_new = jnp.where(jnp.tril(jnp.ones((S, S), dtype=bool)), s_new, -jnp.inf)
    p = jax.nn.softmax(jnp.concatenate([s_past, s_new], axis=-1), axis=-1).astype(v.dtype)
    return jnp.einsum('bhqk,bkhd->bqhd', p[..., :P], v_past) + jnp.einsum('bhqk,bkhd->bqhd', p[..., P:], v)


def setup_inputs(seed: int = 0) -> dict:
    key = jax.random.key(seed)
    ks = jax.random.split(key, 32)
    f32 = jnp.float32
    n_pages = PAST_LEN // PAGE_SIZE
    n_used = DEC_BATCH * n_pages
    n_pool = n_used + n_used // 4
    nrm = lambda k, shape, s=1.0: s * jax.random.normal(k, shape, f32)
    x_prompt = nrm(ks[0], (BATCH, SEQ, D_MODEL))
    x_sample = nrm(ks[1], (DEC_BATCH, DEC_SEQ, D_MODEL))
    state_conv = nrm(ks[2], (N_EVEN, DEC_BATCH, CONV_W - 1, D_CONV))
    state_C = nrm(ks[3], (N_EVEN, DEC_BATCH, MLSTM_HEADS, HD_B, HD_B), 0.05)
    state_n = nrm(ks[4], (N_EVEN, DEC_BATCH, MLSTM_HEADS, HD_B), 0.05)
    state_m = nrm(ks[5], (N_EVEN, DEC_BATCH, MLSTM_HEADS))
    cache_k = nrm(ks[6], (N_ODD, n_pool, PAGE_SIZE, FOX_HEADS, FOX_HEAD_DIM))
    cache_v = nrm(ks[7], (N_ODD, n_pool, PAGE_SIZE, FOX_HEADS, FOX_HEAD_DIM))
    cache_logf = jax.nn.log_sigmoid(FOX_F_BIAS + nrm(ks[8], (N_ODD, n_pool, PAGE_SIZE, FOX_HEADS), 0.5))
    page_table = jax.random.permutation(ks[9], n_pool)[:n_used].reshape(DEC_BATCH, n_pages).astype(jnp.int32)
    meta_tokens = nrm(ks[10], (N_META, D_MODEL))
    gain = lambda k, shape: 1.0 + nrm(k, shape, 0.05)
    norm_mix_pre = gain(ks[11], (DEPTH, D_MODEL))
    norm_mix_post = gain(ks[12], (DEPTH, D_MODEL))
    norm_mlp_pre = gain(ks[13], (DEPTH, D_MODEL))
    norm_mlp_post = gain(ks[14], (DEPTH, D_MODEL))
    w_in_ab = nrm(ks[15], (N_EVEN, D_MODEL, D_IN_AB), D_MODEL ** -0.5)
    conv_w = nrm(ks[16], (N_EVEN, CONV_W, D_CONV), CONV_W ** -0.5)
    b_igate = nrm(ks[17], (N_EVEN, MLSTM_HEADS), 0.1)
    b_fgate = MLSTM_F_BIAS + nrm(ks[18], (N_EVEN, MLSTM_HEADS), 0.5)
    mlstm_norm = gain(ks[19], (N_EVEN, D_MLSTM))
    w_out_ab = nrm(ks[20], (N_EVEN, D_CONV + D_MLSTM, D_MODEL), (D_CONV + D_MLSTM) ** -0.5)
    w_in_c = nrm(ks[21], (N_ODD, D_MODEL, D_IN_C), D_MODEL ** -0.5)
    b_fox = FOX_F_BIAS + nrm(ks[22], (N_ODD, FOX_HEADS), 0.5)
    w_out_c = nrm(ks[23], (N_ODD, D_FOX, D_MODEL), D_FOX ** -0.5)
    w_mlp_up = nrm(ks[24], (DEPTH, D_MODEL, D_FF), D_MODEL ** -0.5)
    w_mlp_down = nrm(ks[25], (DEPTH, D_FF, D_MODEL), D_FF ** -0.5)
    return {'x_prompt': x_prompt, 'x_sample': x_sample, 'state_conv': state_conv, 'state_C': state_C,
            'state_n': state_n, 'state_m': state_m, 'cache_k': cache_k, 'cache_v': cache_v,
            'cache_logf': cache_logf, 'page_table': page_table, 'meta_tokens': meta_tokens,
            'norm_mix_pre': norm_mix_pre, 'norm_mix_post': norm_mix_post, 'norm_mlp_pre': norm_mlp_pre,
            'norm_mlp_post': norm_mlp_post, 'w_in_ab': w_in_ab, 'conv_w': conv_w, 'b_igate': b_igate,
            'b_fgate': b_fgate, 'mlstm_norm': mlstm_norm, 'w_out_ab': w_out_ab, 'w_in_c': w_in_c,
            'b_fox': b_fox, 'w_out_c': w_out_c, 'w_mlp_up': w_mlp_up, 'w_mlp_down': w_mlp_down}


def reference(x_prompt, x_sample, state_conv, state_C, state_n, state_m, cache_k, cache_v, cache_logf,
              page_table, meta_tokens, norm_mix_pre, norm_mix_post, norm_mlp_pre, norm_mlp_post,
              w_in_ab, conv_w, b_igate, b_fgate, mlstm_norm, w_out_ab, w_in_c, b_fox, w_out_c,
              w_mlp_up, w_mlp_down):
    dt = x_prompt.dtype
    f32 = jnp.float32
    bp, bs = x_prompt.shape[0], x_sample.shape[0]
    meta = jnp.broadcast_to(meta_tokens.astype(dt)[None], (bp, N_META, D_MODEL))
    xp = jnp.concatenate([meta, x_prompt], axis=1)
    xs = x_sample
    conv_p, C_p, n_p, m_p, conv_s, C_s, n_s, m_s = [], [], [], [], [], [], [], []
    k_p, v_p, lf_p, k_s, v_s, lf_s = [], [], [], [], [], []
    for layer in range(DEPTH):
        hp = rmsnorm(xp, norm_mix_pre[layer])
        hs = rmsnorm(xs, norm_mix_pre[layer])
        if layer % 2 == 0:
            e = layer // 2
            w = (w_in_ab[e], conv_w[e], b_igate[e], b_fgate[e], mlstm_norm[e], w_out_ab[e])
            zero_state = (jnp.zeros((bp, MLSTM_HEADS, HD_B, HD_B), f32),
                          jnp.zeros((bp, MLSTM_HEADS, HD_B), f32),
                          jnp.full((bp, MLSTM_HEADS), M_INIT, f32))
            mp, cvp, stp = mixer_ab(hp, jnp.zeros((bp, CONV_W - 1, D_CONV), dt), zero_state, *w, N_META)
            past_state = (state_C[e].astype(f32), state_n[e].astype(f32), state_m[e].astype(f32))
            ms, cvs, sts = mixer_ab(hs, state_conv[e].astype(dt), past_state, *w, xs.shape[1])
            conv_p.append(cvp)
            C_p.append(stp[0].astype(dt))
            n_p.append(stp[1].astype(dt))
            m_p.append(stp[2].astype(dt))
            conv_s.append(cvs)
            C_s.append(sts[0].astype(dt))
            n_s.append(sts[1].astype(dt))
            m_s.append(sts[2].astype(dt))
        else:
            o = layer // 2
            qp, kp, vp, lfp = fox_project(hp, w_in_c[o], b_fox[o])
            mp = fox_prompt_attn(qp, kp, vp, lfp).reshape(bp, -1, D_FOX) @ w_out_c[o]
            qs, kn, vn, lfn = fox_project(hs, w_in_c[o], b_fox[o])
            k_past = cache_k[o, page_table].reshape(bs, -1, FOX_HEADS, FOX_HEAD_DIM).astype(dt)
            v_past = cache_v[o, page_table].reshape(bs, -1, FOX_HEADS, FOX_HEAD_DIM).astype(dt)
            lf_past = cache_logf[o, page_table].reshape(bs, -1, FOX_HEADS)
            ms = fox_sample_attn(qs, kn, vn, lfn, k_past, v_past, lf_past).reshape(bs, -1, D_FOX) @ w_out_c[o]
            k_p.append(kp)
            v_p.append(vp)
            lf_p.append(lfp.astype(dt))
            k_s.append(kn)
            v_s.append(vn)
            lf_s.append(lfn.astype(dt))
        xp = xp + rmsnorm(mp, norm_mix_post[layer])
        xs = xs + rmsnorm(ms, norm_mix_post[layer])
        xp = xp + rmsnorm(sq_relu_mlp(rmsnorm(xp, norm_mlp_pre[layer]), w_mlp_up[layer], w_mlp_down[layer]), norm_mlp_post[layer])
        xs = xs + rmsnorm(sq_relu_mlp(rmsnorm(xs, norm_mlp_pre[layer]), w_mlp_up[layer], w_mlp_down[layer]), norm_mlp_post[layer])
    y_prompt = xp[:, N_META:]
    y_sample = xs
    return (y_prompt, y_sample,
            jnp.stack(conv_p), jnp.stack(C_p), jnp.stack(n_p), jnp.stack(m_p),
            jnp.stack(k_p), jnp.stack(v_p), jnp.stack(lf_p),
            jnp.stack(conv_s), jnp.stack(C_s), jnp.stack(n_s), jnp.stack(m_s),
            jnp.stack(k_s), jnp.stack(v_s), jnp.stack(lf_s))
```

```python
import functools

import jax
import jax.numpy as jnp
from jax import lax
from jax.experimental import pallas as pl
from jax.experimental.pallas import tpu as pltpu

F32 = jnp.float32
BF16 = jnp.bfloat16
EPS = 1e-6
M_INIT = -1e30
NEG = -1e30
HIGHEST = lax.Precision.HIGHEST

V7X_VMEM_BYTES = 64 * 1024 * 1024
VMEM_LIMIT = V7X_VMEM_BYTES - 8 * 1024 * 1024
LANES = 128
BF16_SUBLANES = 16

NT_DIMS = (((1,), (1,)), ((), ()))


def _cparams(*sem):
    return pltpu.CompilerParams(dimension_semantics=sem, vmem_limit_bytes=VMEM_LIMIT)


def _round_up(x, m):
    return -(-x // m) * m


def _row_tile(t, target):
    if t <= target:
        return t
    n = pl.cdiv(t, target)
    return _round_up(pl.cdiv(t, n), BF16_SUBLANES)


def _rms(x, g):
    ms = jnp.mean(x * x, axis=-1, keepdims=True)
    return x * lax.rsqrt(ms + EPS) * g


def _div_pow2(x, n):
    assert n & (n - 1) == 0
    return lax.shift_right_logical(x, n.bit_length() - 1)


def _mod_pow2(x, n):
    assert n & (n - 1) == 0
    return lax.bitwise_and(x, n - 1)


def _log_sigmoid(x):
    return jnp.minimum(x, 0.0) - jnp.log1p(jnp.exp(-jnp.abs(x)))


def _rmsnorm_kernel(x_ref, g_ref, o_ref):
    o_ref[...] = _rms(x_ref[...], g_ref[...]).astype(o_ref.dtype)


def rmsnorm(x, g, tile=1024):
    b, t, d = x.shape
    tm = _row_tile(t, tile)
    return pl.pallas_call(
        _rmsnorm_kernel,
        grid=(b, pl.cdiv(t, tm)),
        in_specs=[pl.BlockSpec((None, tm, d), lambda i, j: (i, j, 0)),
                  pl.BlockSpec((1, d), lambda i, j: (0, 0))],
        out_specs=pl.BlockSpec((None, tm, d), lambda i, j: (i, j, 0)),
        out_shape=jax.ShapeDtypeStruct((b, t, d), BF16),
        compiler_params=_cparams("parallel", "parallel"),
        name="rmsnorm",
    )(x, g)


def _matmul_kernel(x_ref, w_ref, o_ref, wb_ref):
    @pl.when((pl.program_id(1) == 0) & (pl.program_id(2) == 0))
    def _():
        wb_ref[...] = w_ref[...].astype(BF16)

    o_ref[...] = jnp.dot(x_ref[...], wb_ref[...], preferred_element_type=F32).astype(o_ref.dtype)


def matmul_cols(x, w, col0, ncols, out_dtype, tn=1024, tile=1376):
    b, t, k = x.shape
    tm = _row_tile(t, tile)
    assert col0 % tn == 0 and ncols % tn == 0
    c0 = col0 // tn
    return pl.pallas_call(
        _matmul_kernel,
        grid=(ncols // tn, b, pl.cdiv(t, tm)),
        in_specs=[pl.BlockSpec((None, tm, k), lambda n, i, j: (i, j, 0)),
                  pl.BlockSpec((k, tn), lambda n, i, j: (0, c0 + n))],
        out_specs=pl.BlockSpec((None, tm, tn), lambda n, i, j: (i, j, n)),
        out_shape=jax.ShapeDtypeStruct((b, t, ncols), out_dtype),
        scratch_shapes=[pltpu.VMEM((k, tn), BF16)],
        compiler_params=_cparams("arbitrary", "arbitrary", "arbitrary"),
        name="matmul_cols",
    )(x, w)


def _gates_kernel(x_ref, wt_ref, b_ref, o_ref, *, n_plain):
    g = lax.dot_general(wt_ref[...], x_ref[...], NT_DIMS, preferred_element_type=F32)
    g = g + b_ref[...]
    row = lax.broadcasted_iota(jnp.int32, g.shape, 0)
    o_ref[...] = jnp.where(row < n_plain, g, _log_sigmoid(g))


def gates(x, wt, bias, n_plain, tile=1024):
    b, t, k = x.shape
    ng = wt.shape[0]
    tm = t if t <= tile else tile
    return pl.pallas_call(
        functools.partial(_gates_kernel, n_plain=n_plain),
        grid=(b, pl.cdiv(t, tm)),
        in_specs=[pl.BlockSpec((None, tm, k), lambda i, j: (i, j, 0)),
                  pl.BlockSpec((ng, k), lambda i, j: (0, 0)),
                  pl.BlockSpec((ng, 1), lambda i, j: (0, 0))],
        out_specs=pl.BlockSpec((None, ng, tm), lambda i, j: (i, 0, j)),
        out_shape=jax.ShapeDtypeStruct((b, ng, t), F32),
        compiler_params=_cparams("parallel", "parallel"),
        name="gates",
    )(x, wt, bias)


def _conv_kernel(ab_ref, ac_ref, ax_ref, prev_ref, w_ref, y_ref, new_ref, carry_ref, *, tail):
    j = pl.program_id(1)

    @pl.when(j == 0)
    def _():
        carry_ref[...] = prev_ref[...]

    u = ac_ref[...].astype(F32) * ax_ref[...].astype(F32)
    tc = u.shape[0]
    row = lax.broadcasted_iota(jnp.int32, u.shape, 0)
    c0 = carry_ref[0:1, :]
    c1 = carry_ref[1:2, :]
    u1 = jnp.where(row == 0, c1, pltpu.roll(u, 1, axis=0))
    u2 = jnp.where(row == 0, c0, jnp.where(row == 1, c1, pltpu.roll(u, 2, axis=0)))
    z = w_ref[0:1, :] * u2 + w_ref[1:2, :] * u1 + w_ref[2:3, :] * u
    y_ref[...] = (ab_ref[...].astype(F32) * z).astype(y_ref.dtype)
    carry_ref[...] = u[tc - 2:tc, :]

    @pl.when(j == pl.num_programs(1) - 1)
    def _():
        new_ref[...] = u[tail:tail + 2, :]


def conv_mixer(p, prev, w, tile=688):
    b, t, _ = p.shape
    c = w.shape[1]
    tc = _row_tile(t, tile)
    nt = pl.cdiv(t, tc)
    tail = (t - 2) - (nt - 1) * tc
    assert 0 <= tail and tail + 2 <= tc
    return pl.pallas_call(
        functools.partial(_conv_kernel, tail=tail),
        grid=(b, nt),
        in_specs=[pl.BlockSpec((None, tc, c), lambda i, j: (i, j, 0)),
                  pl.BlockSpec((None, tc, c), lambda i, j: (i, j, 1)),
                  pl.BlockSpec((None, tc, c), lambda i, j: (i, j, 2)),
                  pl.BlockSpec((None, 2, c), lambda i, j: (i, 0, 0)),
                  pl.BlockSpec((3, c), lambda i, j: (0, 0))],
        out_specs=[pl.BlockSpec((None, tc, c), lambda i, j: (i, j, 0)),
                   pl.BlockSpec((None, 2, c), lambda i, j: (i, 0, 0))],
        out_shape=[jax.ShapeDtypeStruct((b, t, c), BF16),
                   jax.ShapeDtypeStruct((b, 2, c), F32)],
        scratch_shapes=[pltpu.VMEM((2, c), F32)],
        compiler_params=_cparams("parallel", "arbitrary"),
        name="conv_mixer",
    )(p, p, p, prev, w)


def _mlstm_kernel(q_ref, k_ref, v_ref, o_ref, g_ref, c0_ref, n0_ref, m0_ref, nh_ref,
                  y_ref, cout_ref, nout_ref, mout_ref, c_s, n_s, m_s, *, chunk, total, scale):
    ci = pl.program_id(2)

    @pl.when(ci == 0)
    def _():
        c_s[...] = c0_ref[...]
        n_s[...] = n0_ref[...]
        m_s[...] = m0_ref[...]

    lb = q_ref.shape[0]
    valid = total - ci * chunk
    rows = lax.broadcasted_iota(jnp.int32, (chunk, 1), 0)

    def load(ref):
        x = ref[...].astype(F32)
        if lb < chunk:
            x = jnp.concatenate([x, jnp.zeros((chunk - lb, x.shape[1]), F32)], axis=0)
        return jnp.where(rows < valid, x, 0.0)

    q = load(q_ref)
    k = load(k_ref) * scale
    v = load(v_ref)
    qb = q.astype(BF16)
    kb = k.astype(BF16)

    lane = lax.broadcasted_iota(jnp.int32, (1, chunk), 1)
    li = jnp.where(lane < valid, g_ref[0:1, :], NEG)
    lf = jnp.where(lane < valid, g_ref[1:2, :], 0.0)

    ri = lax.broadcasted_iota(jnp.int32, (chunk, chunk), 0)
    cj = lax.broadcasted_iota(jnp.int32, (chunk, chunk), 1)
    causal = cj <= ri
    eye = cj == ri
    b_col = jnp.sum(jnp.where(causal, lf, 0.0), axis=1, keepdims=True)
    b_row = jnp.sum(jnp.where(eye, b_col, 0.0), axis=0, keepdims=True)
    li_col = jnp.sum(jnp.where(eye, li, 0.0), axis=1, keepdims=True)

    m_prev = m_s[...]
    c_prev = c_s[...]
    n_prev = n_s[...]
    d = jnp.where(causal, b_col - b_row + li, NEG)
    g_col = m_prev + b_col
    m_t = jnp.maximum(g_col, jnp.max(d, axis=1, keepdims=True))
    s = lax.dot_general(qb, kb, NT_DIMS, preferred_element_type=F32) * jnp.exp(d - m_t)
    inter = jnp.exp(g_col - m_t)
    cq = lax.dot_general(qb, c_prev.astype(BF16), NT_DIMS, preferred_element_type=F32)
    num = jnp.dot(s.astype(BF16), v.astype(BF16), preferred_element_type=F32) + inter * cq
    den = jnp.sum(s, axis=1, keepdims=True) + inter * jnp.sum(q * n_prev, axis=1, keepdims=True)
    hh = num / jnp.maximum(jnp.abs(den), jnp.exp(-m_t))

    m_new = m_t[chunk - 1:chunk, :]
    b_last = b_row[:, chunk - 1:chunk]
    w_col = jnp.exp(b_last - b_col + li_col - m_new)
    decay = jnp.exp(m_prev + b_last - m_new)
    vw = (v * w_col).T.astype(BF16)
    c_s[...] = decay * c_prev + jnp.dot(vw, kb, preferred_element_type=F32)
    n_s[...] = decay * n_prev + jnp.sum(w_col * k, axis=0, keepdims=True)
    m_s[...] = m_new

    mu = jnp.mean(hh, axis=1, keepdims=True)
    hc = hh - mu
    var = jnp.mean(hc * hc, axis=1, keepdims=True)
    hn = hc * lax.rsqrt(var + EPS) * nh_ref[...]
    y = jax.nn.sigmoid(o_ref[...].astype(F32)) * hn[:lb]
    y_ref[...] = y.astype(y_ref.dtype)

    @pl.when(ci == pl.num_programs(2) - 1)
    def _():
        cout_ref[...] = c_s[...]
        nout_ref[...] = n_s[...]
        mout_ref[...] = m_s[...]


def mlstm_mixer(p, gh, c0, n0, m0, norm_h, col0, chunk):
    b, t, _ = p.shape
    _, h, dv, dk = c0.shape
    lb = min(chunk, t)
    nc = pl.cdiv(t, chunk)
    blk0 = col0 // dk
    scale = float(dk) ** -0.5

    def pspec(group):
        return pl.BlockSpec((None, lb, dk), lambda i, hh, c: (i, c, blk0 + group * h + hh))

    state4 = lambda r, cdim: pl.BlockSpec((None, None, r, cdim), lambda i, hh, c: (i, hh, 0, 0))
    return pl.pallas_call(
        functools.partial(_mlstm_kernel, chunk=chunk, total=t, scale=scale),
        grid=(b, h, nc),
        in_specs=[pspec(0), pspec(1), pspec(2), pspec(3),
                  pl.BlockSpec((None, None, 2, chunk), lambda i, hh, c: (i, hh, 0, c)),
                  state4(dv, dk), state4(1, dk), state4(1, 1),
                  pl.BlockSpec((1, dv), lambda i, hh, c: (0, hh))],
        out_specs=[pl.BlockSpec((None, lb, dv), lambda i, hh, c: (i, c, hh)),
                   state4(dv, dk), state4(1, dk), state4(1, 1)],
        out_shape=[jax.ShapeDtypeStruct((b, t, h * dv), BF16),
                   jax.ShapeDtypeStruct((b, h, dv, dk), F32),
                   jax.ShapeDtypeStruct((b, h, 1, dk), F32),
                   jax.ShapeDtypeStruct((b, h, 1, 1), F32)],
        scratch_shapes=[pltpu.VMEM((dv, dk), F32), pltpu.VMEM((1, dk), F32), pltpu.VMEM((1, 1), F32)],
        compiler_params=_cparams("parallel", "parallel", "arbitrary"),
        name="mlstm_mixer",
    )(p, p, p, p, gh, c0, n0, m0, norm_h)


def _outproj_kernel(*refs, n_in):
    a_refs = refs[:n_in]
    w_ref, g_ref, x_ref, o_ref = refs[n_in:]
    y = None
    k0 = 0
    for a_ref in a_refs:
        kk = a_ref.shape[1]
        part = jnp.dot(a_ref[...], w_ref[k0:k0 + kk, :], preferred_element_type=F32)
        y = part if y is None else y + part
        k0 += kk
    o_ref[...] = x_ref[...] + _rms(y, g_ref[...])


def outproj_residual(a_list, w, g, x, tile=688):
    b, t, d = x.shape
    tm = _row_tile(t, tile)
    kt = w.shape[0]
    a_specs = [pl.BlockSpec((None, tm, a.shape[2]), lambda i, j: (i, j, 0)) for a in a_list]
    return pl.pallas_call(
        functools.partial(_outproj_kernel, n_in=len(a_list)),
        grid=(b, pl.cdiv(t, tm)),
        in_specs=a_specs + [pl.BlockSpec((kt, d), lambda i, j: (0, 0)),
                            pl.BlockSpec((1, d), lambda i, j: (0, 0)),
                            pl.BlockSpec((None, tm, d), lambda i, j: (i, j, 0))],
        out_specs=pl.BlockSpec((None, tm, d), lambda i, j: (i, j, 0)),
        out_shape=jax.ShapeDtypeStruct((b, t, d), F32),
        compiler_params=_cparams("parallel", "parallel"),
        name="outproj_residual",
    )(*a_list, w, g, x)


def _mlp_kernel(x_ref, g1_ref, wu_ref, wd_ref, g2_ref, g3_ref, o_ref, on_ref, xn_s, acc_s):
    f = pl.program_id(2)

    @pl.when(f == 0)
    def _():
        xn_s[...] = _rms(x_ref[...], g1_ref[...]).astype(BF16)
        acc_s[...] = jnp.zeros_like(acc_s)

    h = jnp.maximum(jnp.dot(xn_s[...], wu_ref[...], preferred_element_type=F32), 0.0)
    acc_s[...] += jnp.dot((h * h).astype(BF16), wd_ref[...], preferred_element_type=F32)

    @pl.when(f == pl.num_programs(2) - 1)
    def _():
        xnew = x_ref[...] + _rms(acc_s[...], g2_ref[...])
        o_ref[...] = xnew
        on_ref[...] = _rms(xnew, g3_ref[...]).astype(on_ref.dtype)


def mlp_residual(x, g_pre, w_up, w_down, g_post, g_next, tile=688, tf=512):
    b, t, d = x.shape
    f = w_up.shape[1]
    tm = _row_tile(t, tile)
    row = pl.BlockSpec((None, tm, d), lambda i, j, l: (i, j, 0))
    vec = pl.BlockSpec((1, d), lambda i, j, l: (0, 0))
    return pl.pallas_call(
        _mlp_kernel,
        grid=(b, pl.cdiv(t, tm), f // tf),
        in_specs=[row, vec,
                  pl.BlockSpec((d, tf), lambda i, j, l: (0, l)),
                  pl.BlockSpec((tf, d), lambda i, j, l: (l, 0)),
                  vec, vec],
        out_specs=[row, row],
        out_shape=[jax.ShapeDtypeStruct((b, t, d), F32), jax.ShapeDtypeStruct((b, t, d), BF16)],
        scratch_shapes=[pltpu.VMEM((tm, d), BF16), pltpu.VMEM((tm, d), F32)],
        compiler_params=_cparams("parallel", "parallel", "arbitrary"),
        name="mlp_residual",
    )(x, g_pre, w_up, w_down, g_post, g_next)


def _fox_cumsum_kernel(lf_ref, crow_ref, ccol_ref, carry_ref):
    j = pl.program_id(1)

    @pl.when(j == 0)
    def _():
        carry_ref[...] = jnp.zeros_like(carry_ref)

    nh, tc = lf_ref.shape
    ri = lax.broadcasted_iota(jnp.int32, (tc, tc), 0)
    cj = lax.broadcasted_iota(jnp.int32, (tc, tc), 1)
    causal = cj <= ri
    eye = cj == ri
    lane = lax.broadcasted_iota(jnp.int32, (tc, LANES), 1)
    cols = jnp.zeros((tc, LANES), F32)
    for h in range(nh):
        col = jnp.sum(jnp.where(causal, lf_ref[h:h + 1, :], 0.0), axis=1, keepdims=True)
        col = col + carry_ref[h:h + 1, :]
        crow_ref[h:h + 1, :] = jnp.sum(jnp.where(eye, col, 0.0), axis=0, keepdims=True)
        carry_ref[h:h + 1, :] = col[tc - 1:tc, :]
        cols = jnp.where(lane == h, col, cols)
    ccol_ref[...] = cols


def fox_cumsum(lf, tile=512):
    b, nh, t = lf.shape
    tc = tile
    return pl.pallas_call(
        _fox_cumsum_kernel,
        grid=(b, pl.cdiv(t, tc)),
        in_specs=[pl.BlockSpec((None, nh, tc), lambda i, j: (i, 0, j))],
        out_specs=[pl.BlockSpec((None, nh, tc), lambda i, j: (i, 0, j)),
                   pl.BlockSpec((None, tc, LANES), lambda i, j: (i, j, 0))],
        out_shape=[jax.ShapeDtypeStruct((b, nh, t), F32), jax.ShapeDtypeStruct((b, t, LANES), F32)],
        scratch_shapes=[pltpu.VMEM((nh, 1), F32)],
        compiler_params=_cparams("parallel", "arbitrary"),
        name="fox_cumsum",
    )(lf)


def _fox_prompt_kernel(q_ref, k_ref, v_ref, cq_ref, ck_ref, o_ref, m_s, l_s, acc_s, *, nh, hd, total, scale):
    qi = pl.program_id(1)
    ki = pl.program_id(2)
    tq = q_ref.shape[0]
    tk = k_ref.shape[0]

    @pl.when(ki == 0)
    def _():
        m_s[...] = jnp.full_like(m_s, NEG)
        l_s[...] = jnp.zeros_like(l_s)
        acc_s[...] = jnp.zeros_like(acc_s)

    @pl.when(ki <= qi)
    def _():
        qpos = qi * tq + lax.broadcasted_iota(jnp.int32, (tq, tk), 0)
        kpos = ki * tk + lax.broadcasted_iota(jnp.int32, (tq, tk), 1)
        visible = kpos <= qpos
        krow_ok = (ki * tk + lax.broadcasted_iota(jnp.int32, (tk, 1), 0)) < total
        for h in range(nh):
            sl = slice(h * hd, (h + 1) * hd)
            kh = k_ref[:, sl].astype(BF16)
            vh = jnp.where(krow_ok, v_ref[:, sl], 0.0).astype(BF16)
            s = lax.dot_general(q_ref[:, sl], kh, NT_DIMS, preferred_element_type=F32) * scale
            s = s + cq_ref[:, h:h + 1] - ck_ref[h:h + 1, :]
            s = jnp.where(visible, s, NEG)
            m_prev = m_s[:, h:h + 1]
            m_new = jnp.maximum(m_prev, jnp.max(s, axis=1, keepdims=True))
            alpha = jnp.exp(m_prev - m_new)
            p = jnp.exp(s - m_new)
            l_s[:, h:h + 1] = alpha * l_s[:, h:h + 1] + jnp.sum(p, axis=1, keepdims=True)
            acc_s[:, sl] = alpha * acc_s[:, sl] + jnp.dot(p.astype(BF16), vh, preferred_element_type=F32)
            m_s[:, h:h + 1] = m_new

    @pl.when(ki == qi)
    def _():
        for h in range(nh):
            sl = slice(h * hd, (h + 1) * hd)
            o_ref[:, sl] = (acc_s[:, sl] / l_s[:, h:h + 1]).astype(o_ref.dtype)


def fox_prompt_attention(q, k, v, c_col, c_row, nh, tile=512):
    b, t, d = q.shape
    hd = d // nh
    tq = tk = tile
    nq = pl.cdiv(t, tq)
    kv_spec = pl.BlockSpec((None, tk, d), lambda i, a, c: (i, jnp.minimum(c, a), 0))
    return pl.pallas_call(
        functools.partial(_fox_prompt_kernel, nh=nh, hd=hd, total=t, scale=float(hd) ** -0.5),
        grid=(b, nq, nq),
        in_specs=[pl.BlockSpec((None, tq, d), lambda i, a, c: (i, a, 0)),
                  kv_spec, kv_spec,
                  pl.BlockSpec((None, tq, LANES), lambda i, a, c: (i, a, 0)),
                  pl.BlockSpec((None, nh, tk), lambda i, a, c: (i, 0, jnp.minimum(c, a)))],
        out_specs=pl.BlockSpec((None, tq, d), lambda i, a, c: (i, a, 0)),
        out_shape=jax.ShapeDtypeStruct((b, t, d), BF16),
        scratch_shapes=[pltpu.VMEM((tq, LANES), F32), pltpu.VMEM((tq, LANES), F32), pltpu.VMEM((tq, d), F32)],
        compiler_params=_cparams("parallel", "parallel", "arbitrary"),
        name="fox_prompt_attention",
    )(q, k, v, c_col, c_row)


def _fox_sample_kernel(pt_ref, q_ref, kn_ref, vn_ref, lfn_ref, *refs, nh, hd, npp, scale):
    del pt_ref
    k_refs = refs[:npp]
    v_refs = refs[npp:2 * npp]
    lf_refs = refs[2 * npp:3 * npp]
    o_ref, qbd_s, ccol_s, m_s, l_s, acc_s, carry_s = refs[3 * npp:]
    j = pl.program_id(1)
    s_new = q_ref.shape[0]
    page = k_refs[0].shape[0]
    nr = nh * s_new
    d = nh * hd

    expand = (_div_pow2(lax.broadcasted_iota(jnp.int32, (nr, nh), 0), s_new)
              == lax.broadcasted_iota(jnp.int32, (nr, nh), 1)).astype(F32)

    def update(s, v_b):
        m_prev = m_s[...]
        m_new = jnp.maximum(m_prev, jnp.max(s, axis=1, keepdims=True))
        alpha = jnp.exp(m_prev - m_new)
        p = jnp.exp(s - m_new)
        l_s[...] = alpha * l_s[...] + jnp.sum(p, axis=1, keepdims=True)
        acc_s[...] = alpha * acc_s[...] + jnp.dot(p.astype(BF16), v_b, preferred_element_type=F32)
        m_s[...] = m_new

    @pl.when(j == 0)
    def _():
        qrep = jnp.concatenate([q_ref[...].astype(F32)] * nh, axis=0)
        rh = _div_pow2(lax.broadcasted_iota(jnp.int32, (nr, d), 0), s_new)
        ch = _div_pow2(lax.broadcasted_iota(jnp.int32, (nr, d), 1), hd)
        qbd_s[...] = jnp.where(rh == ch, qrep, 0.0).astype(BF16)
        carry_s[...] = jnp.zeros_like(carry_s)
        m_s[...] = jnp.full_like(m_s, NEG)
        l_s[...] = jnp.zeros_like(l_s)
        acc_s[...] = jnp.zeros_like(acc_s)

        si = lax.broadcasted_iota(jnp.int32, (LANES, LANES), 0)
        sj = lax.broadcasted_iota(jnp.int32, (LANES, LANES), 1)
        upper = (si <= sj).astype(F32)
        c_new = jnp.dot(lfn_ref[...], upper, preferred_element_type=F32, precision=HIGHEST)
        c_b = jnp.dot(expand, c_new, preferred_element_type=F32, precision=HIGHEST)
        tok = _mod_pow2(lax.broadcasted_iota(jnp.int32, (nr, LANES), 0), s_new)
        key = lax.broadcasted_iota(jnp.int32, (nr, LANES), 1)
        ccol = jnp.sum(jnp.where(key == tok, c_b, 0.0), axis=1, keepdims=True)
        ccol_s[...] = ccol
        pad = jnp.zeros((LANES - s_new, d), F32)
        kn = jnp.concatenate([kn_ref[...], pad], axis=0).astype(BF16)
        vn = jnp.concatenate([vn_ref[...], pad], axis=0).astype(BF16)
        s = lax.dot_general(qbd_s[...], kn, NT_DIMS, preferred_element_type=F32) * scale
        s = jnp.where((key < s_new) & (key <= tok), s + ccol - c_b, NEG)
        update(s, vn)

    pi = lax.broadcasted_iota(jnp.int32, (page, page), 0)
    pj = lax.broadcasted_iota(jnp.int32, (page, page), 1)
    later = (pj > pi).astype(F32)
    s_parts, v_parts = [], []
    for i in range(npp):
        lfp = lf_refs[i][...]
        rev = jnp.dot(later, lfp, preferred_element_type=F32, precision=HIGHEST) + carry_s[...]
        carry_s[...] = carry_s[...] + jnp.sum(lfp, axis=0, keepdims=True)
        bias = lax.dot_general(expand, rev, NT_DIMS, preferred_element_type=F32, precision=HIGHEST)
        kb = k_refs[i][...].astype(BF16)
        s = lax.dot_general(qbd_s[...], kb, NT_DIMS, preferred_element_type=F32) * scale
        s_parts.append(s + ccol_s[...] + bias)
        v_parts.append(v_refs[i][...].astype(BF16))
    update(jnp.concatenate(s_parts, axis=1), jnp.concatenate(v_parts, axis=0))

    @pl.when(j == pl.num_programs(1) - 1)
    def _():
        for h in range(nh):
            rs = slice(h * s_new, (h + 1) * s_new)
            cs = slice(h * hd, (h + 1) * hd)
            o_ref[:, cs] = (acc_s[rs, cs] / l_s[rs, :]).astype(o_ref.dtype)


def fox_sample_attention(q, k_new, v_new, lf_new, cache_k, cache_v, cache_lf, page_table, layer, nh, npp=4):
    bd, s_new, d = q.shape
    hd = d // nh
    page = cache_k.shape[2]
    npages = page_table.shape[1]
    assert npages % npp == 0 and nh * s_new == LANES and page == LANES
    nr = nh * s_new

    def paged(width, i):
        return pl.BlockSpec((None, None, page, width),
                            lambda b, j, pt: (layer, pt[b, npages - 1 - (j * npp + i)], 0, 0))

    seq = lambda r, w: pl.BlockSpec((None, r, w), lambda b, j, pt: (b, 0, 0))
    grid_spec = pltpu.PrefetchScalarGridSpec(
        num_scalar_prefetch=1,
        grid=(bd, npages // npp),
        in_specs=[seq(s_new, d), seq(s_new, d), seq(s_new, d), seq(nh, LANES)]
                 + [paged(d, i) for i in range(npp)]
                 + [paged(d, i) for i in range(npp)]
                 + [paged(nh, i) for i in range(npp)],
        out_specs=seq(s_new, d),
        scratch_shapes=[pltpu.VMEM((nr, d), BF16), pltpu.VMEM((nr, 1), F32), pltpu.VMEM((nr, 1), F32),
                        pltpu.VMEM((nr, 1), F32), pltpu.VMEM((nr, d), F32), pltpu.VMEM((1, nh), F32)],
    )
    return pl.pallas_call(
        functools.partial(_fox_sample_kernel, nh=nh, hd=hd, npp=npp, scale=float(hd) ** -0.5),
        grid_spec=grid_spec,
        out_shape=jax.ShapeDtypeStruct((bd, s_new, d), BF16),
        compiler_params=_cparams("parallel", "arbitrary"),
        name="fox_sample_attention",
    )(page_table, q, k_new, v_new, lf_new, *([cache_k] * npp), *([cache_v] * npp), *([cache_lf] * npp))


def kernel(x_prompt, x_sample, state_conv, state_C, state_n, state_m, cache_k, cache_v, cache_logf,
           page_table, meta_tokens, norm_mix_pre, norm_mix_post, norm_mlp_pre, norm_mlp_post,
           w_in_ab, conv_w, b_igate, b_fgate, mlstm_norm, w_out_ab, w_in_c, b_fox, w_out_c,
           w_mlp_up, w_mlp_down):
    bp, seq, d_model = x_prompt.shape
    bs, s_new, _ = x_sample.shape
    depth = norm_mix_pre.shape[0]
    n_meta = meta_tokens.shape[0]
    _, _, mh, hdb, _ = state_C.shape
    d_mlstm = mh * hdb
    d_conv = conv_w.shape[2]
    n_odd, n_pool, page, fh, fhd = cache_k.shape
    d_fox = fh * fhd
    tp = n_meta + seq

    meta = jnp.broadcast_to(meta_tokens[None], (bp, n_meta, d_model))
    xp = jnp.concatenate([meta, x_prompt], axis=1)
    xs = x_sample.reshape(1, bs * s_new, d_model)
    vec = lambda a: a.reshape(1, -1)

    ck = cache_k.reshape(n_odd, n_pool, page, d_fox)
    cv = cache_v.reshape(n_odd, n_pool, page, d_fox)

    xpn = rmsnorm(xp, vec(norm_mix_pre[0]))
    xsn = rmsnorm(xs, vec(norm_mix_pre[0]))

    conv_p, c_p, n_p, m_p, conv_s, c_s, n_s, m_s = [], [], [], [], [], [], [], []
    k_p, v_p, lf_p, k_s, v_s, lf_s = [], [], [], [], [], []
    for layer in range(depth):
        if layer % 2 == 0:
            e = layer // 2
            n_main = 3 * d_conv + 4 * d_mlstm
            w_in = w_in_ab[e]
            wg_t = w_in[:, n_main:].T.astype(BF16)
            b_g = jnp.concatenate([b_igate[e], b_fgate[e]]).reshape(2 * mh, 1)
            w_out = w_out_ab[e].astype(BF16)
            nh_vec = vec(mlstm_norm[e])

            def mix(xn, prev, c0, n0, m0, chunk, nb):
                t = xn.shape[1]
                proj = matmul_cols(xn, w_in, 0, n_main, BF16)
                g = gates(xn, wg_t, b_g, mh)
                proj = proj.reshape(nb, t // (nb // xn.shape[0]), n_main)
                tl = proj.shape[1]
                g = g.reshape(xn.shape[0], 2, mh, nb // xn.shape[0], tl)
                gh = jnp.transpose(g, (0, 3, 2, 1, 4)).reshape(nb, mh, 2, tl)
                if tl < chunk:
                    gh = jnp.pad(gh, ((0, 0), (0, 0), (0, 0), (0, chunk - tl)))
                ya, cnew = conv_mixer(proj, prev, conv_w[e])
                yb, c1, n1, m1 = mlstm_mixer(proj, gh, c0, n0.reshape(nb, mh, 1, hdb), m0.reshape(nb, mh, 1, 1),
                                             nh_vec, 3 * d_conv, chunk)
                shp = (xn.shape[0], t, -1)
                return ya.reshape(shp), yb.reshape(shp), cnew, c1, n1.reshape(nb, mh, hdb), m1.reshape(nb, mh)

            zc = jnp.zeros((bp, mh, hdb, hdb), F32)
            ya, yb, cvp, c1, n1, m1 = mix(xpn, jnp.zeros((bp, 2, d_conv), F32), zc,
                                          jnp.zeros((bp, mh, hdb), F32), jnp.full((bp, mh), M_INIT, F32), 256, bp)
            conv_p.append(cvp); c_p.append(c1); n_p.append(n1); m_p.append(m1)
            xp = outproj_residual([ya, yb], w_out, vec(norm_mix_post[layer]), xp)
            ya, yb, cvs, c1, n1, m1 = mix(xsn, state_conv[e], state_C[e], state_n[e], state_m[e], LANES, bs)
            conv_s.append(cvs); c_s.append(c1); n_s.append(n1); m_s.append(m1)
            xs = outproj_residual([ya, yb], w_out, vec(norm_mix_post[layer]), xs)
        else:
            o = layer // 2
            w_in = w_in_c[o]
            wg_t = w_in[:, 3 * d_fox:].T.astype(BF16)
            b_g = b_fox[o].reshape(fh, 1)
            w_out = w_out_c[o].astype(BF16)

            q = matmul_cols(xpn, w_in, 0, d_fox, BF16)
            k = matmul_cols(xpn, w_in, d_fox, d_fox, F32)
            v = matmul_cols(xpn, w_in, 2 * d_fox, d_fox, F32)
            lf = gates(xpn, wg_t, b_g, 0)
            c_row, c_col = fox_cumsum(lf)
            att = fox_prompt_attention(q, k, v, c_col, c_row, fh)
            k_p.append(k.reshape(bp, tp, fh, fhd)); v_p.append(v.reshape(bp, tp, fh, fhd))
            lf_p.append(jnp.transpose(lf, (0, 2, 1)))
            xp = outproj_residual([att], w_out, vec(norm_mix_post[layer]), xp)

            q = matmul_cols(xsn, w_in, 0, d_fox, BF16).reshape(bs, s_new, d_fox)
            k = matmul_cols(xsn, w_in, d_fox, d_fox, F32).reshape(bs, s_new, d_fox)
            v = matmul_cols(xsn, w_in, 2 * d_fox, d_fox, F32).reshape(bs, s_new, d_fox)
            lf = gates(xsn, wg_t, b_g, 0)
            lf = jnp.transpose(lf.reshape(fh, bs, s_new), (1, 0, 2))
            lf_pad = jnp.pad(lf, ((0, 0), (0, 0), (0, LANES - s_new)))
            att = fox_sample_attention(q, k, v, lf_pad, ck, cv, cache_logf, page_table, o, fh)
            k_s.append(k.reshape(bs, s_new, fh, fhd)); v_s.append(v.reshape(bs, s_new, fh, fhd))
            lf_s.append(jnp.transpose(lf, (0, 2, 1)))
            xs = outproj_residual([att.reshape(1, bs * s_new, d_fox)], w_out, vec(norm_mix_post[layer]), xs)

        g_next = vec(norm_mix_pre[layer + 1]) if layer + 1 < depth else vec(norm_mix_pre[layer])
        wu = w_mlp_up[layer].astype(BF16)
        wd = w_mlp_down[layer].astype(BF16)
        xp, xpn = mlp_residual(xp, vec(norm_mlp_pre[layer]), wu, wd, vec(norm_mlp_post[layer]), g_next)
        xs, xsn = mlp_residual(xs, vec(norm_mlp_pre[layer]), wu, wd, vec(norm_mlp_post[layer]), g_next)

    y_prompt = xp[:, n_meta:]
    y_sample = xs.reshape(bs, s_new, d_model)
    return (y_prompt, y_sample,
            jnp.stack(conv_p), jnp.stack(c_p), jnp.stack(n_p), jnp.stack(m_p),
            jnp.stack(k_p), jnp.stack(v_p), jnp.stack(lf_p),
            jnp.stack(conv_s), jnp.stack(c_s), jnp.stack(n_s), jnp.stack(m_s),
            jnp.stack(k_s), jnp.stack(v_s), jnp.stack(lf_s))
```

```python
import functools

import jax
import jax.numpy as jnp
from jax import lax
from jax.experimental import pallas as pl
from jax.experimental.pallas import tpu as pltpu

F32 = jnp.float32
BF16 = jnp.bfloat16
EPS = 1e-6
M_INIT = -1e30
NEG = -1e30
HIGHEST = lax.Precision.HIGHEST
LOG2E = 1.4426950408889634

V7X_VMEM_BYTES = 64 * 1024 * 1024
VMEM_LIMIT = V7X_VMEM_BYTES - 8 * 1024 * 1024
LANES = 128
BF16_SUBLANES = 16

NT_DIMS = (((1,), (1,)), ((), ()))


def _cparams(*sem):
    return pltpu.CompilerParams(dimension_semantics=sem, vmem_limit_bytes=VMEM_LIMIT)


def _round_up(x, m):
    return -(-x // m) * m


def _row_tile(t, target, align=BF16_SUBLANES):
    if t <= target:
        return t
    n = pl.cdiv(t, target)
    return _round_up(pl.cdiv(t, n), align)


def _rms(x, g):
    ms = jnp.mean(x * x, axis=-1, keepdims=True)
    return x * lax.rsqrt(ms + EPS) * g


def _div_pow2(x, n):
    assert n & (n - 1) == 0
    return lax.shift_right_logical(x, n.bit_length() - 1)


def _mod_pow2(x, n):
    assert n & (n - 1) == 0
    return lax.bitwise_and(x, n - 1)


def _log_sigmoid(x):
    return jnp.minimum(x, 0.0) - jnp.log1p(jnp.exp(-jnp.abs(x)))


def _rmsnorm_kernel(x_ref, g_ref, o_ref):
    o_ref[...] = _rms(x_ref[...], g_ref[...]).astype(o_ref.dtype)


def rmsnorm(x, g, tile=1024):
    b, t, d = x.shape
    tm = _row_tile(t, tile)
    return pl.pallas_call(
        _rmsnorm_kernel,
        grid=(b, pl.cdiv(t, tm)),
        in_specs=[pl.BlockSpec((None, tm, d), lambda i, j: (i, j, 0)),
                  pl.BlockSpec((1, d), lambda i, j: (0, 0))],
        out_specs=pl.BlockSpec((None, tm, d), lambda i, j: (i, j, 0)),
        out_shape=jax.ShapeDtypeStruct((b, t, d), BF16),
        compiler_params=_cparams("parallel", "parallel"),
        name="rmsnorm",
    )(x, g)


def _matmul_kernel(x_ref, w_ref, o_ref, *rest):
    wb_ref = rest[-1]

    @pl.when((pl.program_id(1) == 0) & (pl.program_id(2) == 0))
    def _():
        wb_ref[...] = w_ref[...].astype(BF16)

    y = jnp.dot(x_ref[...], wb_ref[...], preferred_element_type=F32)
    o_ref[...] = y.astype(o_ref.dtype)
    if len(rest) == 2:
        rest[0][...] = y.T.astype(rest[0].dtype)


def matmul_cols(x, w, col0, ncols, out_dtype, tn=1024, tile=1376, transposed_copy=False):
    b, t, k = x.shape
    tm = _row_tile(t, tile, LANES if transposed_copy else BF16_SUBLANES)
    assert col0 % tn == 0 and ncols % tn == 0
    c0 = col0 // tn
    out_specs = [pl.BlockSpec((None, tm, tn), lambda n, i, j: (i, j, n))]
    out_shape = [jax.ShapeDtypeStruct((b, t, ncols), out_dtype)]
    if transposed_copy:
        assert tm % LANES == 0
        out_specs.append(pl.BlockSpec((None, tn, tm), lambda n, i, j: (i, n, j)))
        out_shape.append(jax.ShapeDtypeStruct((b, ncols, t), BF16))
    res = pl.pallas_call(
        _matmul_kernel,
        grid=(ncols // tn, b, pl.cdiv(t, tm)),
        in_specs=[pl.BlockSpec((None, tm, k), lambda n, i, j: (i, j, 0)),
                  pl.BlockSpec((k, tn), lambda n, i, j: (0, c0 + n))],
        out_specs=out_specs,
        out_shape=out_shape,
        scratch_shapes=[pltpu.VMEM((k, tn), BF16)],
        compiler_params=_cparams("arbitrary", "arbitrary", "arbitrary"),
        name="matmul_cols",
    )(x, w)
    return res if transposed_copy else res[0]


def _gates_kernel(x_ref, wt_ref, b_ref, o_ref, *, n_plain):
    g = lax.dot_general(wt_ref[...], x_ref[...], NT_DIMS, preferred_element_type=F32)
    g = g + b_ref[...]
    row = lax.broadcasted_iota(jnp.int32, g.shape, 0)
    o_ref[...] = jnp.where(row < n_plain, g, _log_sigmoid(g))


def gates(x, wt, bias, n_plain, tile=1024):
    b, t, k = x.shape
    ng = wt.shape[0]
    tm = t if t <= tile else tile
    return pl.pallas_call(
        functools.partial(_gates_kernel, n_plain=n_plain),
        grid=(b, pl.cdiv(t, tm)),
        in_specs=[pl.BlockSpec((None, tm, k), lambda i, j: (i, j, 0)),
                  pl.BlockSpec((ng, k), lambda i, j: (0, 0)),
                  pl.BlockSpec((ng, 1), lambda i, j: (0, 0))],
        out_specs=pl.BlockSpec((None, ng, tm), lambda i, j: (i, 0, j)),
        out_shape=jax.ShapeDtypeStruct((b, ng, t), F32),
        compiler_params=_cparams("parallel", "parallel"),
        name="gates",
    )(x, wt, bias)


def _conv_kernel(ab_ref, ac_ref, ax_ref, prev_ref, w_ref, y_ref, new_ref, carry_ref, *, tail):
    j = pl.program_id(1)

    @pl.when(j == 0)
    def _():
        carry_ref[...] = prev_ref[...]

    u = ac_ref[...].astype(F32) * ax_ref[...].astype(F32)
    tc = u.shape[0]
    row = lax.broadcasted_iota(jnp.int32, u.shape, 0)
    c0 = carry_ref[0:1, :]
    c1 = carry_ref[1:2, :]
    u1 = jnp.where(row == 0, c1, pltpu.roll(u, 1, axis=0))
    u2 = jnp.where(row == 0, c0, jnp.where(row == 1, c1, pltpu.roll(u, 2, axis=0)))
    z = w_ref[0:1, :] * u2 + w_ref[1:2, :] * u1 + w_ref[2:3, :] * u
    y_ref[...] = (ab_ref[...].astype(F32) * z).astype(y_ref.dtype)
    carry_ref[...] = u[tc - 2:tc, :]

    @pl.when(j == pl.num_programs(1) - 1)
    def _():
        new_ref[...] = u[tail:tail + 2, :]


def conv_mixer(p, prev, w, tile=688):
    b, t, _ = p.shape
    c = w.shape[1]
    tc = _row_tile(t, tile)
    nt = pl.cdiv(t, tc)
    tail = (t - 2) - (nt - 1) * tc
    assert 0 <= tail and tail + 2 <= tc
    return pl.pallas_call(
        functools.partial(_conv_kernel, tail=tail),
        grid=(b, nt),
        in_specs=[pl.BlockSpec((None, tc, c), lambda i, j: (i, j, 0)),
                  pl.BlockSpec((None, tc, c), lambda i, j: (i, j, 1)),
                  pl.BlockSpec((None, tc, c), lambda i, j: (i, j, 2)),
                  pl.BlockSpec((None, 2, c), lambda i, j: (i, 0, 0)),
                  pl.BlockSpec((3, c), lambda i, j: (0, 0))],
        out_specs=[pl.BlockSpec((None, tc, c), lambda i, j: (i, j, 0)),
                   pl.BlockSpec((None, 2, c), lambda i, j: (i, 0, 0))],
        out_shape=[jax.ShapeDtypeStruct((b, t, c), BF16),
                   jax.ShapeDtypeStruct((b, 2, c), F32)],
        scratch_shapes=[pltpu.VMEM((2, c), F32)],
        compiler_params=_cparams("parallel", "arbitrary"),
        name="conv_mixer",
    )(p, p, p, prev, w)


def _mlstm_kernel(q_ref, k_ref, v_ref, o_ref, g_ref, c0_ref, n0_ref, m0_ref, nh_ref,
                  y_ref, cout_ref, nout_ref, mout_ref, c_s, n_s, m_s, *, chunk, total, scale):
    ci = pl.program_id(2)

    @pl.when(ci == 0)
    def _():
        c_s[...] = c0_ref[...]
        n_s[...] = n0_ref[...]
        m_s[...] = m0_ref[...]

    lb = q_ref.shape[0]
    valid = total - ci * chunk
    rows = lax.broadcasted_iota(jnp.int32, (chunk, 1), 0)

    def load(ref):
        x = ref[...].astype(F32)
        if lb < chunk:
            x = jnp.concatenate([x, jnp.zeros((chunk - lb, x.shape[1]), F32)], axis=0)
        return jnp.where(rows < valid, x, 0.0)

    q = load(q_ref)
    k = load(k_ref) * scale
    v = load(v_ref)
    qb = q.astype(BF16)
    kb = k.astype(BF16)

    lane = lax.broadcasted_iota(jnp.int32, (1, chunk), 1)
    li = jnp.where(lane < valid, g_ref[0:1, :], NEG)
    lf = jnp.where(lane < valid, g_ref[1:2, :], 0.0)

    ri = lax.broadcasted_iota(jnp.int32, (chunk, chunk), 0)
    cj = lax.broadcasted_iota(jnp.int32, (chunk, chunk), 1)
    causal = cj <= ri
    eye = cj == ri
    b_col = jnp.sum(jnp.where(causal, lf, 0.0), axis=1, keepdims=True)
    b_row = jnp.sum(jnp.where(eye, b_col, 0.0), axis=0, keepdims=True)
    li_col = jnp.sum(jnp.where(eye, li, 0.0), axis=1, keepdims=True)

    m_prev = m_s[...]
    c_prev = c_s[...]
    n_prev = n_s[...]
    d = jnp.where(causal, b_col - b_row + li, NEG)
    g_col = m_prev + b_col
    m_t = jnp.maximum(g_col, jnp.max(d, axis=1, keepdims=True))
    s = lax.dot_general(qb, kb, NT_DIMS, preferred_element_type=F32) * jnp.exp(d - m_t)
    inter = jnp.exp(g_col - m_t)
    cq = lax.dot_general(qb, c_prev.astype(BF16), NT_DIMS, preferred_element_type=F32)
    num = jnp.dot(s.astype(BF16), v.astype(BF16), preferred_element_type=F32) + inter * cq
    den = jnp.sum(s, axis=1, keepdims=True) + inter * jnp.sum(q * n_prev, axis=1, keepdims=True)
    hh = num / jnp.maximum(jnp.abs(den), jnp.exp(-m_t))

    m_new = m_t[chunk - 1:chunk, :]
    b_last = b_row[:, chunk - 1:chunk]
    w_col = jnp.exp(b_last - b_col + li_col - m_new)
    decay = jnp.exp(m_prev + b_last - m_new)
    vw = (v * w_col).T.astype(BF16)
    c_s[...] = decay * c_prev + jnp.dot(vw, kb, preferred_element_type=F32)
    n_s[...] = decay * n_prev + jnp.sum(w_col * k, axis=0, keepdims=True)
    m_s[...] = m_new

    mu = jnp.mean(hh, axis=1, keepdims=True)
    hc = hh - mu
    var = jnp.mean(hc * hc, axis=1, keepdims=True)
    hn = hc * lax.rsqrt(var + EPS) * nh_ref[...]
    y = jax.nn.sigmoid(o_ref[...].astype(F32)) * hn[:lb]
    y_ref[...] = y.astype(y_ref.dtype)

    @pl.when(ci == pl.num_programs(2) - 1)
    def _():
        cout_ref[...] = c_s[...]
        nout_ref[...] = n_s[...]
        mout_ref[...] = m_s[...]


def mlstm_mixer(p, gh, c0, n0, m0, norm_h, col0, chunk):
    b, t, _ = p.shape
    _, h, dv, dk = c0.shape
    lb = min(chunk, t)
    nc = pl.cdiv(t, chunk)
    blk0 = col0 // dk
    scale = float(dk) ** -0.5

    def pspec(group):
        return pl.BlockSpec((None, lb, dk), lambda i, hh, c: (i, c, blk0 + group * h + hh))

    state4 = lambda r, cdim: pl.BlockSpec((None, None, r, cdim), lambda i, hh, c: (i, hh, 0, 0))
    return pl.pallas_call(
        functools.partial(_mlstm_kernel, chunk=chunk, total=t, scale=scale),
        grid=(b, h, nc),
        in_specs=[pspec(0), pspec(1), pspec(2), pspec(3),
                  pl.BlockSpec((None, None, 2, chunk), lambda i, hh, c: (i, hh, 0, c)),
                  state4(dv, dk), state4(1, dk), state4(1, 1),
                  pl.BlockSpec((1, dv), lambda i, hh, c: (0, hh))],
        out_specs=[pl.BlockSpec((None, lb, dv), lambda i, hh, c: (i, c, hh)),
                   state4(dv, dk), state4(1, dk), state4(1, 1)],
        out_shape=[jax.ShapeDtypeStruct((b, t, h * dv), BF16),
                   jax.ShapeDtypeStruct((b, h, dv, dk), F32),
                   jax.ShapeDtypeStruct((b, h, 1, dk), F32),
                   jax.ShapeDtypeStruct((b, h, 1, 1), F32)],
        scratch_shapes=[pltpu.VMEM((dv, dk), F32), pltpu.VMEM((1, dk), F32), pltpu.VMEM((1, 1), F32)],
        compiler_params=_cparams("parallel", "parallel", "arbitrary"),
        name="mlstm_mixer",
    )(p, p, p, p, gh, c0, n0, m0, norm_h)


def _outproj_kernel(*refs, n_in):
    a_refs = refs[:n_in]
    w_ref, g_ref, x_ref, o_ref = refs[n_in:]
    y = None
    k0 = 0
    for a_ref in a_refs:
        kk = a_ref.shape[1]
        part = jnp.dot(a_ref[...], w_ref[k0:k0 + kk, :], preferred_element_type=F32)
        y = part if y is None else y + part
        k0 += kk
    o_ref[...] = x_ref[...] + _rms(y, g_ref[...])


def outproj_residual(a_list, w, g, x, tile=688):
    b, t, d = x.shape
    tm = _row_tile(t, tile)
    kt = w.shape[0]
    a_specs = [pl.BlockSpec((None, tm, a.shape[2]), lambda i, j: (i, j, 0)) for a in a_list]
    return pl.pallas_call(
        functools.partial(_outproj_kernel, n_in=len(a_list)),
        grid=(b, pl.cdiv(t, tm)),
        in_specs=a_specs + [pl.BlockSpec((kt, d), lambda i, j: (0, 0)),
                            pl.BlockSpec((1, d), lambda i, j: (0, 0)),
                            pl.BlockSpec((None, tm, d), lambda i, j: (i, j, 0))],
        out_specs=pl.BlockSpec((None, tm, d), lambda i, j: (i, j, 0)),
        out_shape=jax.ShapeDtypeStruct((b, t, d), F32),
        compiler_params=_cparams("parallel", "parallel"),
        name="outproj_residual",
    )(*a_list, w, g, x)


def _mlp_kernel(x_ref, g1_ref, wu_ref, wd_ref, g2_ref, g3_ref, o_ref, on_ref, xn_s, acc_s):
    f = pl.program_id(2)

    @pl.when(f == 0)
    def _():
        xn_s[...] = _rms(x_ref[...], g1_ref[...]).astype(BF16)
        acc_s[...] = jnp.zeros_like(acc_s)

    h = jnp.maximum(jnp.dot(xn_s[...], wu_ref[...], preferred_element_type=F32), 0.0)
    acc_s[...] += jnp.dot((h * h).astype(BF16), wd_ref[...], preferred_element_type=F32)

    @pl.when(f == pl.num_programs(2) - 1)
    def _():
        xnew = x_ref[...] + _rms(acc_s[...], g2_ref[...])
        o_ref[...] = xnew
        on_ref[...] = _rms(xnew, g3_ref[...]).astype(on_ref.dtype)


def mlp_residual(x, g_pre, w_up, w_down, g_post, g_next, tile=688, tf=512):
    b, t, d = x.shape
    f = w_up.shape[1]
    tm = _row_tile(t, tile)
    row = pl.BlockSpec((None, tm, d), lambda i, j, l: (i, j, 0))
    vec = pl.BlockSpec((1, d), lambda i, j, l: (0, 0))
    return pl.pallas_call(
        _mlp_kernel,
        grid=(b, pl.cdiv(t, tm), f // tf),
        in_specs=[row, vec,
                  pl.BlockSpec((d, tf), lambda i, j, l: (0, l)),
                  pl.BlockSpec((tf, d), lambda i, j, l: (l, 0)),
                  vec, vec],
        out_specs=[row, row],
        out_shape=[jax.ShapeDtypeStruct((b, t, d), F32), jax.ShapeDtypeStruct((b, t, d), BF16)],
        scratch_shapes=[pltpu.VMEM((tm, d), BF16), pltpu.VMEM((tm, d), F32)],
        compiler_params=_cparams("parallel", "parallel", "arbitrary"),
        name="mlp_residual",
    )(x, g_pre, w_up, w_down, g_post, g_next)


def _fox_cumsum_kernel(lf_ref, crow_ref, ccol_ref, carry_ref):
    j = pl.program_id(1)

    @pl.when(j == 0)
    def _():
        carry_ref[...] = jnp.zeros_like(carry_ref)

    nh, tc = lf_ref.shape
    ri = lax.broadcasted_iota(jnp.int32, (tc, tc), 0)
    cj = lax.broadcasted_iota(jnp.int32, (tc, tc), 1)
    causal = cj <= ri
    eye = cj == ri
    lane = lax.broadcasted_iota(jnp.int32, (tc, LANES), 1)
    cols = jnp.zeros((tc, LANES), F32)
    for h in range(nh):
        col = jnp.sum(jnp.where(causal, lf_ref[h:h + 1, :], 0.0), axis=1, keepdims=True)
        col = col + carry_ref[h:h + 1, :]
        crow_ref[h:h + 1, :] = jnp.sum(jnp.where(eye, col, 0.0), axis=0, keepdims=True) * LOG2E
        carry_ref[h:h + 1, :] = col[tc - 1:tc, :]
        cols = jnp.where(lane == h, col, cols)
    ccol_ref[...] = cols * LOG2E


def fox_cumsum(lf, tile=512):
    b, nh, t = lf.shape
    tc = tile
    return pl.pallas_call(
        _fox_cumsum_kernel,
        grid=(b, pl.cdiv(t, tc)),
        in_specs=[pl.BlockSpec((None, nh, tc), lambda i, j: (i, 0, j))],
        out_specs=[pl.BlockSpec((None, nh, tc), lambda i, j: (i, 0, j)),
                   pl.BlockSpec((None, tc, LANES), lambda i, j: (i, j, 0))],
        out_shape=[jax.ShapeDtypeStruct((b, nh, t), F32), jax.ShapeDtypeStruct((b, t, LANES), F32)],
        scratch_shapes=[pltpu.VMEM((nh, 1), F32)],
        compiler_params=_cparams("parallel", "arbitrary"),
        name="fox_cumsum",
    )(lf)


def _fox_prompt_kernel(q_ref, k_ref, vt_ref, cq_ref, ck_ref, o_ref, m_s, l_s, acc_s, *, nh, hd, total, scale):
    qi = pl.program_id(1)
    ki = pl.program_id(2)
    tq = q_ref.shape[0]
    tk = k_ref.shape[0]

    @pl.when(ki == 0)
    def _():
        m_s[...] = jnp.full_like(m_s, NEG)
        l_s[...] = jnp.zeros_like(l_s)
        acc_s[...] = jnp.zeros_like(acc_s)

    def step(diagonal):
        if diagonal:
            kpos = ki * tk + lax.broadcasted_iota(jnp.int32, (tk, tq), 0)
            qpos = qi * tq + lax.broadcasted_iota(jnp.int32, (tk, tq), 1)
            visible = kpos <= qpos
            vcol_ok = (ki * tk + lax.broadcasted_iota(jnp.int32, (1, tk), 1)) < total
        for h in range(nh):
            sl = slice(h * hd, (h + 1) * hd)
            st = lax.dot_general(k_ref[:, sl].astype(BF16), q_ref[:, sl], NT_DIMS, preferred_element_type=F32)
            st = st * (scale * LOG2E) + cq_ref[h:h + 1, :] - ck_ref[:, h:h + 1]
            vt = vt_ref[sl, :]
            if diagonal:
                st = jnp.where(visible, st, NEG)
                vt = jnp.where(vcol_ok, vt, jnp.zeros_like(vt))
            m_prev = m_s[h:h + 1, :]
            m_new = jnp.maximum(m_prev, jnp.max(st, axis=0, keepdims=True))
            alpha = jnp.exp2(m_prev - m_new)
            p = jnp.exp2(st - m_new)
            l_s[h:h + 1, :] = alpha * l_s[h:h + 1, :] + jnp.sum(p, axis=0, keepdims=True)
            acc_s[sl, :] = alpha * acc_s[sl, :] + jnp.dot(vt, p.astype(BF16), preferred_element_type=F32)
            m_s[h:h + 1, :] = m_new

    @pl.when(ki < qi)
    def _():
        step(False)

    @pl.when(ki == qi)
    def _():
        step(True)
        for h in range(nh):
            sl = slice(h * hd, (h + 1) * hd)
            o_ref[:, sl] = (acc_s[sl, :] / l_s[h:h + 1, :]).T.astype(o_ref.dtype)


def fox_prompt_attention(q, k, vt, c_row, c_col, nh, tile=512):
    b, t, d = q.shape
    hd = d // nh
    tq = tk = tile
    nq = pl.cdiv(t, tq)
    kmap = lambda i, a, c: (i, jnp.minimum(c, a), 0)
    return pl.pallas_call(
        functools.partial(_fox_prompt_kernel, nh=nh, hd=hd, total=t, scale=float(hd) ** -0.5),
        grid=(b, nq, nq),
        in_specs=[pl.BlockSpec((None, tq, d), lambda i, a, c: (i, a, 0)),
                  pl.BlockSpec((None, tk, d), kmap),
                  pl.BlockSpec((None, d, tk), lambda i, a, c: (i, 0, jnp.minimum(c, a))),
                  pl.BlockSpec((None, nh, tq), lambda i, a, c: (i, 0, a)),
                  pl.BlockSpec((None, tk, LANES), kmap)],
        out_specs=pl.BlockSpec((None, tq, d), lambda i, a, c: (i, a, 0)),
        out_shape=jax.ShapeDtypeStruct((b, t, d), BF16),
        scratch_shapes=[pltpu.VMEM((nh, tq), F32), pltpu.VMEM((nh, tq), F32), pltpu.VMEM((d, tq), F32)],
        compiler_params=_cparams("parallel", "parallel", "arbitrary"),
        name="fox_prompt_attention",
    )(q, k, vt, c_row, c_col)


def _fox_sample_kernel(pt_ref, q_ref, kn_ref, vn_ref, lfn_ref, *refs, nh, hd, npp, scale):
    del pt_ref
    k_refs = refs[:npp]
    v_refs = refs[npp:2 * npp]
    lf_refs = refs[2 * npp:3 * npp]
    o_ref, qbdt_s, crow_s, m_s, l_s, acc_s, carry_s = refs[3 * npp:]
    j = pl.program_id(1)
    s_new = q_ref.shape[0]
    page = lf_refs[0].shape[0]
    nr = nh * s_new
    d = nh * hd
    npair = nh // 2

    expand_t = (lax.broadcasted_iota(jnp.int32, (nh, nr), 0)
                == _div_pow2(lax.broadcasted_iota(jnp.int32, (nh, nr), 1), s_new)).astype(F32)
    eye_r = lax.broadcasted_iota(jnp.int32, (nr, nr), 0) == lax.broadcasted_iota(jnp.int32, (nr, nr), 1)

    def head_rows(ref, h):
        return ref[pl.ds(h, page, stride=nh), :]

    def update(st, v_pairs):
        m_prev = m_s[...]
        m_new = jnp.maximum(m_prev, jnp.max(st, axis=0, keepdims=True))
        alpha = jnp.exp(m_prev - m_new)
        pt = jnp.exp(st - m_new)
        l_s[...] = alpha * l_s[...] + jnp.sum(pt, axis=0, keepdims=True)
        m_s[...] = m_new
        p = pt.T.astype(BF16)
        alpha_col = jnp.sum(jnp.where(eye_r, alpha, 0.0), axis=1, keepdims=True)
        for c in range(npair):
            rs = slice(2 * s_new * c, 2 * s_new * (c + 1))
            acc_s[rs, :] = alpha_col[rs, :] * acc_s[rs, :] + jnp.dot(p[rs, :], v_pairs[c], preferred_element_type=F32)

    @pl.when(j == 0)
    def _():
        qrep = jnp.concatenate([q_ref[...].astype(F32)] * nh, axis=0)
        rh = _div_pow2(lax.broadcasted_iota(jnp.int32, (nr, d), 0), s_new)
        ch = _div_pow2(lax.broadcasted_iota(jnp.int32, (nr, d), 1), hd)
        qbdt_s[...] = jnp.where(rh == ch, qrep, 0.0).T.astype(BF16)
        carry_s[...] = jnp.zeros_like(carry_s)
        m_s[...] = jnp.full_like(m_s, NEG)
        l_s[...] = jnp.zeros_like(l_s)
        acc_s[...] = jnp.zeros_like(acc_s)

        si = lax.broadcasted_iota(jnp.int32, (LANES, LANES), 0)
        sj = lax.broadcasted_iota(jnp.int32, (LANES, LANES), 1)
        lower = (sj <= si).astype(F32)
        c_new_t = lax.dot_general(lower, lfn_ref[...], NT_DIMS, preferred_element_type=F32, precision=HIGHEST)
        c_b = jnp.dot(c_new_t, expand_t, preferred_element_type=F32, precision=HIGHEST)
        key = lax.broadcasted_iota(jnp.int32, (LANES, nr), 0)
        tok = _mod_pow2(lax.broadcasted_iota(jnp.int32, (LANES, nr), 1), s_new)
        crow = jnp.sum(jnp.where(key == tok, c_b, 0.0), axis=0, keepdims=True)
        crow_s[...] = crow
        pad = jnp.zeros((LANES - s_new, d), F32)
        kn = jnp.concatenate([kn_ref[...], pad], axis=0).astype(BF16)
        vn = jnp.concatenate([vn_ref[...], pad], axis=0).astype(BF16)
        st = jnp.dot(kn, qbdt_s[...], preferred_element_type=F32) * scale
        st = jnp.where((key < s_new) & (key <= tok), st + crow - c_b, NEG)
        update(st, [vn[:, 2 * hd * c:2 * hd * (c + 1)] for c in range(npair)])

    pi = lax.broadcasted_iota(jnp.int32, (page, page), 0)
    pj = lax.broadcasted_iota(jnp.int32, (page, page), 1)
    later = (pj > pi).astype(F32)
    st_parts = []
    for i in range(npp):
        lfp = lf_refs[i][...]
        rev = jnp.dot(later, lfp, preferred_element_type=F32, precision=HIGHEST) + carry_s[...]
        carry_s[...] = carry_s[...] + jnp.sum(lfp, axis=0, keepdims=True)
        bias = jnp.dot(rev, expand_t, preferred_element_type=F32, precision=HIGHEST)
        kb = jnp.concatenate([head_rows(k_refs[i], h).astype(BF16) for h in range(nh)], axis=1)
        st = jnp.dot(kb, qbdt_s[...], preferred_element_type=F32) * scale
        st_parts.append(st + crow_s[...] + bias)
    v_pairs = []
    for c in range(npair):
        v_pairs.append(jnp.concatenate(
            [jnp.concatenate([head_rows(v_refs[i], 2 * c).astype(BF16),
                              head_rows(v_refs[i], 2 * c + 1).astype(BF16)], axis=1) for i in range(npp)], axis=0))
    update(jnp.concatenate(st_parts, axis=0), v_pairs)

    @pl.when(j == pl.num_programs(1) - 1)
    def _():
        l_col = jnp.sum(jnp.where(eye_r, l_s[...], 0.0), axis=1, keepdims=True)
        for h in range(nh):
            rs = slice(h * s_new, (h + 1) * s_new)
            cs = slice((h % 2) * hd, (h % 2 + 1) * hd)
            o_ref[:, h * hd:(h + 1) * hd] = (acc_s[rs, cs] / l_col[rs, :]).astype(o_ref.dtype)


def fox_sample_attention(q, k_new, v_new, lf_new, cache_k, cache_v, cache_lf, page_table, layer, nh, npp=4):
    bd, s_new, d = q.shape
    hd = d // nh
    page = cache_lf.shape[2]
    npages = page_table.shape[1]
    assert npages % npp == 0 and nh * s_new == LANES and page == LANES and nh % 2 == 0
    nr = nh * s_new

    def paged(rows, width, i):
        return pl.BlockSpec((None, None, rows, width),
                            lambda b, j, pt: (layer, pt[b, npages - 1 - (j * npp + i)], 0, 0))

    seq = lambda r, w: pl.BlockSpec((None, r, w), lambda b, j, pt: (b, 0, 0))
    grid_spec = pltpu.PrefetchScalarGridSpec(
        num_scalar_prefetch=1,
        grid=(bd, npages // npp),
        in_specs=[seq(s_new, d), seq(s_new, d), seq(s_new, d), seq(nh, LANES)]
                 + [paged(page * nh, hd, i) for i in range(npp)]
                 + [paged(page * nh, hd, i) for i in range(npp)]
                 + [paged(page, nh, i) for i in range(npp)],
        out_specs=seq(s_new, d),
        scratch_shapes=[pltpu.VMEM((d, nr), BF16), pltpu.VMEM((1, nr), F32), pltpu.VMEM((1, nr), F32),
                        pltpu.VMEM((1, nr), F32), pltpu.VMEM((nr, 2 * hd), F32), pltpu.VMEM((1, nh), F32)],
    )
    return pl.pallas_call(
        functools.partial(_fox_sample_kernel, nh=nh, hd=hd, npp=npp, scale=float(hd) ** -0.5),
        grid_spec=grid_spec,
        out_shape=jax.ShapeDtypeStruct((bd, s_new, d), BF16),
        compiler_params=_cparams("parallel", "arbitrary"),
        name="fox_sample_attention",
    )(page_table, q, k_new, v_new, lf_new, *([cache_k] * npp), *([cache_v] * npp), *([cache_lf] * npp))


def kernel(x_prompt, x_sample, state_conv, state_C, state_n, state_m, cache_k, cache_v, cache_logf,
           page_table, meta_tokens, norm_mix_pre, norm_mix_post, norm_mlp_pre, norm_mlp_post,
           w_in_ab, conv_w, b_igate, b_fgate, mlstm_norm, w_out_ab, w_in_c, b_fox, w_out_c,
           w_mlp_up, w_mlp_down):
    bp, seq, d_model = x_prompt.shape
    bs, s_new, _ = x_sample.shape
    depth = norm_mix_pre.shape[0]
    n_meta = meta_tokens.shape[0]
    _, _, mh, hdb, _ = state_C.shape
    d_mlstm = mh * hdb
    d_conv = conv_w.shape[2]
    n_odd, n_pool, page, fh, fhd = cache_k.shape
    d_fox = fh * fhd
    tp = n_meta + seq

    meta = jnp.broadcast_to(meta_tokens[None], (bp, n_meta, d_model))
    xp = jnp.concatenate([meta, x_prompt], axis=1)
    xs = x_sample.reshape(1, bs * s_new, d_model)
    vec = lambda a: a.reshape(1, -1)

    ck = cache_k.reshape(n_odd, n_pool, page * fh, fhd)
    cv = cache_v.reshape(n_odd, n_pool, page * fh, fhd)

    xpn = rmsnorm(xp, vec(norm_mix_pre[0]))
    xsn = rmsnorm(xs, vec(norm_mix_pre[0]))

    conv_p, c_p, n_p, m_p, conv_s, c_s, n_s, m_s = [], [], [], [], [], [], [], []
    k_p, v_p, lf_p, k_s, v_s, lf_s = [], [], [], [], [], []
    for layer in range(depth):
        if layer % 2 == 0:
            e = layer // 2
            n_main = 3 * d_conv + 4 * d_mlstm
            w_in = w_in_ab[e]
            wg_t = w_in[:, n_main:].T.astype(BF16)
            b_g = jnp.concatenate([b_igate[e], b_fgate[e]]).reshape(2 * mh, 1)
            w_out = w_out_ab[e].astype(BF16)
            nh_vec = vec(mlstm_norm[e])

            def mix(xn, prev, c0, n0, m0, chunk, nb):
                t = xn.shape[1]
                proj = matmul_cols(xn, w_in, 0, n_main, BF16)
                g = gates(xn, wg_t, b_g, mh)
                proj = proj.reshape(nb, t // (nb // xn.shape[0]), n_main)
                tl = proj.shape[1]
                g = g.reshape(xn.shape[0], 2, mh, nb // xn.shape[0], tl)
                gh = jnp.transpose(g, (0, 3, 2, 1, 4)).reshape(nb, mh, 2, tl)
                if tl < chunk:
                    gh = jnp.pad(gh, ((0, 0), (0, 0), (0, 0), (0, chunk - tl)))
                ya, cnew = conv_mixer(proj, prev, conv_w[e])
                yb, c1, n1, m1 = mlstm_mixer(proj, gh, c0, n0.reshape(nb, mh, 1, hdb), m0.reshape(nb, mh, 1, 1),
                                             nh_vec, 3 * d_conv, chunk)
                shp = (xn.shape[0], t, -1)
                return ya.reshape(shp), yb.reshape(shp), cnew, c1, n1.reshape(nb, mh, hdb), m1.reshape(nb, mh)

            zc = jnp.zeros((bp, mh, hdb, hdb), F32)
            ya, yb, cvp, c1, n1, m1 = mix(xpn, jnp.zeros((bp, 2, d_conv), F32), zc,
                                          jnp.zeros((bp, mh, hdb), F32), jnp.full((bp, mh), M_INIT, F32), 256, bp)
            conv_p.append(cvp); c_p.append(c1); n_p.append(n1); m_p.append(m1)
            xp = outproj_residual([ya, yb], w_out, vec(norm_mix_post[layer]), xp)
            ya, yb, cvs, c1, n1, m1 = mix(xsn, state_conv[e], state_C[e], state_n[e], state_m[e], LANES, bs)
            conv_s.append(cvs); c_s.append(c1); n_s.append(n1); m_s.append(m1)
            xs = outproj_residual([ya, yb], w_out, vec(norm_mix_post[layer]), xs)
        else:
            o = layer // 2
            w_in = w_in_c[o]
            wg_t = w_in[:, 3 * d_fox:].T.astype(BF16)
            b_g = b_fox[o].reshape(fh, 1)
            w_out = w_out_c[o].astype(BF16)

            q = matmul_cols(xpn, w_in, 0, d_fox, BF16)
            k = matmul_cols(xpn, w_in, d_fox, d_fox, F32)
            v, vt = matmul_cols(xpn, w_in, 2 * d_fox, d_fox, F32, tile=1408, transposed_copy=True)
            lf = gates(xpn, wg_t, b_g, 0)
            c_row, c_col = fox_cumsum(lf)
            att = fox_prompt_attention(q, k, vt, c_row, c_col, fh)
            k_p.append(k.reshape(bp, tp, fh, fhd)); v_p.append(v.reshape(bp, tp, fh, fhd))
            lf_p.append(jnp.transpose(lf, (0, 2, 1)))
            xp = outproj_residual([att], w_out, vec(norm_mix_post[layer]), xp)

            q = matmul_cols(xsn, w_in, 0, d_fox, BF16).reshape(bs, s_new, d_fox)
            k = matmul_cols(xsn, w_in, d_fox, d_fox, F32).reshape(bs, s_new, d_fox)
            v = matmul_cols(xsn, w_in, 2 * d_fox, d_fox, F32).reshape(bs, s_new, d_fox)
            lf = gates(xsn, wg_t, b_g, 0)
            lf = jnp.transpose(lf.reshape(fh, bs, s_new), (1, 0, 2))
            lf_pad = jnp.pad(lf, ((0, 0), (0, 0), (0, LANES - s_new)))
            att = fox_sample_attention(q, k, v, lf_pad, ck, cv, cache_logf, page_table, o, fh)
            k_s.append(k.reshape(bs, s_new, fh, fhd)); v_s.append(v.reshape(bs, s_new, fh, fhd))
            lf_s.append(jnp.transpose(lf, (0, 2, 1)))
            xs = outproj_residual([att.reshape(1, bs * s_new, d_fox)], w_out, vec(norm_mix_post[layer]), xs)

        g_next = vec(norm_mix_pre[layer + 1]) if layer + 1 < depth else vec(norm_mix_pre[layer])
        wu = w_mlp_up[layer].astype(BF16)
        wd = w_mlp_down[layer].astype(BF16)
        xp, xpn = mlp_residual(xp, vec(norm_mlp_pre[layer]), wu, wd, vec(norm_mlp_post[layer]), g_next)
        xs, xsn = mlp_residual(xs, vec(norm_mlp_pre[layer]), wu, wd, vec(norm_mlp_post[layer]), g_next)

    y_prompt = xp[:, n_meta:]
    y_sample = xs.reshape(bs, s_new, d_model)
    return (y_prompt, y_sample,
            jnp.stack(conv_p), jnp.stack(c_p), jnp.stack(n_p), jnp.stack(m_p),
            jnp.stack(k_p), jnp.stack(v_p), jnp.stack(lf_p),
            jnp.stack(conv_s), jnp.stack(c_s), jnp.stack(n_s), jnp.stack(m_s),
            jnp.stack(k_s), jnp.stack(v_s), jnp.stack(lf_s))
```

```python
import functools

import jax
import jax.numpy as jnp
from jax import lax
from jax.experimental import pallas as pl
from jax.experimental.pallas import tpu as pltpu

F32 = jnp.float32
BF16 = jnp.bfloat16
EPS = 1e-6
M_INIT = -1e30
NEG = -1e30
HIGHEST = lax.Precision.HIGHEST
LOG2E = 1.4426950408889634

V7X_VMEM_BYTES = 64 * 1024 * 1024
VMEM_LIMIT = V7X_VMEM_BYTES - 8 * 1024 * 1024
LANES = 128
BF16_SUBLANES = 16

NT_DIMS = (((1,), (1,)), ((), ()))


def _cparams(*sem):
    return pltpu.CompilerParams(dimension_semantics=sem, vmem_limit_bytes=VMEM_LIMIT)


def _round_up(x, m):
    return -(-x // m) * m


def _row_tile(t, target, align=BF16_SUBLANES):
    if t <= target:
        return t
    n = pl.cdiv(t, target)
    return _round_up(pl.cdiv(t, n), align)


def _rms(x, g):
    ms = jnp.mean(x * x, axis=-1, keepdims=True)
    return x * lax.rsqrt(ms + EPS) * g


def _div_pow2(x, n):
    assert n & (n - 1) == 0
    return lax.shift_right_logical(x, n.bit_length() - 1)


def _mod_pow2(x, n):
    assert n & (n - 1) == 0
    return lax.bitwise_and(x, n - 1)


def _log_sigmoid(x):
    return jnp.minimum(x, 0.0) - jnp.log1p(jnp.exp(-jnp.abs(x)))


def _rmsnorm_kernel(x_ref, g_ref, o_ref):
    o_ref[...] = _rms(x_ref[...], g_ref[...]).astype(o_ref.dtype)


def rmsnorm(x, g, tile=1024):
    b, t, d = x.shape
    tm = _row_tile(t, tile)
    return pl.pallas_call(
        _rmsnorm_kernel,
        grid=(b, pl.cdiv(t, tm)),
        in_specs=[pl.BlockSpec((None, tm, d), lambda i, j: (i, j, 0)),
                  pl.BlockSpec((1, d), lambda i, j: (0, 0))],
        out_specs=pl.BlockSpec((None, tm, d), lambda i, j: (i, j, 0)),
        out_shape=jax.ShapeDtypeStruct((b, t, d), BF16),
        compiler_params=_cparams("parallel", "parallel"),
        name="rmsnorm",
    )(x, g)


def _matmul_kernel(x_ref, wt_ref, o_ref, *rest):
    wb_ref = rest[-1]

    @pl.when((pl.program_id(1) == 0) & (pl.program_id(2) == 0))
    def _():
        wb_ref[...] = wt_ref[...].astype(BF16)

    y = lax.dot_general(x_ref[...], wb_ref[...], NT_DIMS, preferred_element_type=F32)
    o_ref[...] = y.astype(o_ref.dtype)
    if len(rest) == 2:
        rest[0][...] = y.T.astype(rest[0].dtype)


def matmul_cols(x, wt, layer, col0, ncols, out_dtype, tn=1024, tile=1376, transposed_copy=False):
    b, t, k = x.shape
    tm = _row_tile(t, tile, LANES if transposed_copy else BF16_SUBLANES)
    assert col0 % tn == 0 and ncols % tn == 0
    c0 = col0 // tn
    out_specs = [pl.BlockSpec((None, tm, tn), lambda n, i, j: (i, j, n))]
    out_shape = [jax.ShapeDtypeStruct((b, t, ncols), out_dtype)]
    if transposed_copy:
        assert tm % LANES == 0
        out_specs.append(pl.BlockSpec((None, tn, tm), lambda n, i, j: (i, n, j)))
        out_shape.append(jax.ShapeDtypeStruct((b, ncols, t), BF16))
    res = pl.pallas_call(
        _matmul_kernel,
        grid=(ncols // tn, b, pl.cdiv(t, tm)),
        in_specs=[pl.BlockSpec((None, tm, k), lambda n, i, j: (i, j, 0)),
                  pl.BlockSpec((None, tn, k), lambda n, i, j: (layer, c0 + n, 0))],
        out_specs=out_specs,
        out_shape=out_shape,
        scratch_shapes=[pltpu.VMEM((tn, k), BF16)],
        compiler_params=_cparams("arbitrary", "arbitrary", "arbitrary"),
        name="matmul_cols",
    )(x, wt)
    return res if transposed_copy else res[0]


def _gates_kernel(x_ref, wt_ref, b_ref, o_ref, *, n_plain):
    g = lax.dot_general(wt_ref[...].astype(BF16), x_ref[...], NT_DIMS, preferred_element_type=F32)
    g = g + b_ref[...]
    row = lax.broadcasted_iota(jnp.int32, g.shape, 0)
    o_ref[...] = jnp.where(row < n_plain, g, _log_sigmoid(g))


def gates(x, wt, bias, n_plain, tile=1024):
    b, t, k = x.shape
    ng = wt.shape[0]
    tm = t if t <= tile else tile
    return pl.pallas_call(
        functools.partial(_gates_kernel, n_plain=n_plain),
        grid=(b, pl.cdiv(t, tm)),
        in_specs=[pl.BlockSpec((None, tm, k), lambda i, j: (i, j, 0)),
                  pl.BlockSpec((ng, k), lambda i, j: (0, 0)),
                  pl.BlockSpec((ng, 1), lambda i, j: (0, 0))],
        out_specs=pl.BlockSpec((None, ng, tm), lambda i, j: (i, 0, j)),
        out_shape=jax.ShapeDtypeStruct((b, ng, t), F32),
        compiler_params=_cparams("parallel", "parallel"),
        name="gates",
    )(x, wt, bias)


def _conv_kernel(ab_ref, ac_ref, ax_ref, prev_ref, w_ref, y_ref, new_ref, carry_ref, *, tail):
    j = pl.program_id(1)

    @pl.when(j == 0)
    def _():
        carry_ref[...] = prev_ref[...]

    u = ac_ref[...].astype(F32) * ax_ref[...].astype(F32)
    tc = u.shape[0]
    row = lax.broadcasted_iota(jnp.int32, u.shape, 0)
    c0 = carry_ref[0:1, :]
    c1 = carry_ref[1:2, :]
    u1 = jnp.where(row == 0, c1, pltpu.roll(u, 1, axis=0))
    u2 = jnp.where(row == 0, c0, jnp.where(row == 1, c1, pltpu.roll(u, 2, axis=0)))
    z = w_ref[0:1, :] * u2 + w_ref[1:2, :] * u1 + w_ref[2:3, :] * u
    y_ref[...] = (ab_ref[...].astype(F32) * z).astype(y_ref.dtype)
    carry_ref[...] = u[tc - 2:tc, :]

    @pl.when(j == pl.num_programs(1) - 1)
    def _():
        new_ref[...] = u[tail:tail + 2, :]


def conv_mixer(p, prev, w, tile=688):
    b, t, _ = p.shape
    c = w.shape[1]
    tc = _row_tile(t, tile)
    nt = pl.cdiv(t, tc)
    tail = (t - 2) - (nt - 1) * tc
    assert 0 <= tail and tail + 2 <= tc
    return pl.pallas_call(
        functools.partial(_conv_kernel, tail=tail),
        grid=(b, nt),
        in_specs=[pl.BlockSpec((None, tc, c), lambda i, j: (i, j, 0)),
                  pl.BlockSpec((None, tc, c), lambda i, j: (i, j, 1)),
                  pl.BlockSpec((None, tc, c), lambda i, j: (i, j, 2)),
                  pl.BlockSpec((None, 2, c), lambda i, j: (i, 0, 0)),
                  pl.BlockSpec((3, c), lambda i, j: (0, 0))],
        out_specs=[pl.BlockSpec((None, tc, c), lambda i, j: (i, j, 0)),
                   pl.BlockSpec((None, 2, c), lambda i, j: (i, 0, 0))],
        out_shape=[jax.ShapeDtypeStruct((b, t, c), BF16),
                   jax.ShapeDtypeStruct((b, 2, c), F32)],
        scratch_shapes=[pltpu.VMEM((2, c), F32)],
        compiler_params=_cparams("parallel", "arbitrary"),
        name="conv_mixer",
    )(p, p, p, prev, w)


def _mlstm_kernel(q_ref, k_ref, v_ref, o_ref, g_ref, c0_ref, n0_ref, m0_ref, nh_ref,
                  y_ref, cout_ref, nout_ref, mout_ref, c_s, n_s, m_s, *, chunk, total, scale):
    ci = pl.program_id(2)

    @pl.when(ci == 0)
    def _():
        c_s[...] = c0_ref[...]
        n_s[...] = n0_ref[...]
        m_s[...] = m0_ref[...]

    lb = q_ref.shape[0]
    valid = total - ci * chunk
    rows = lax.broadcasted_iota(jnp.int32, (chunk, 1), 0)

    def load(ref):
        x = ref[...].astype(F32)
        if lb < chunk:
            x = jnp.concatenate([x, jnp.zeros((chunk - lb, x.shape[1]), F32)], axis=0)
        return jnp.where(rows < valid, x, 0.0)

    q = load(q_ref)
    k = load(k_ref) * scale
    v = load(v_ref)
    qb = q.astype(BF16)
    kb = k.astype(BF16)

    lane = lax.broadcasted_iota(jnp.int32, (1, chunk), 1)
    li = jnp.where(lane < valid, g_ref[0:1, :], NEG)
    lf = jnp.where(lane < valid, g_ref[1:2, :], 0.0)

    ri = lax.broadcasted_iota(jnp.int32, (chunk, chunk), 0)
    cj = lax.broadcasted_iota(jnp.int32, (chunk, chunk), 1)
    causal = cj <= ri
    eye = cj == ri
    b_col = jnp.sum(jnp.where(causal, lf, 0.0), axis=1, keepdims=True)
    b_row = jnp.sum(jnp.where(eye, b_col, 0.0), axis=0, keepdims=True)
    li_col = jnp.sum(jnp.where(eye, li, 0.0), axis=1, keepdims=True)

    m_prev = m_s[...]
    c_prev = c_s[...]
    n_prev = n_s[...]
    d = jnp.where(causal, b_col - b_row + li, NEG)
    g_col = m_prev + b_col
    m_t = jnp.maximum(g_col, jnp.max(d, axis=1, keepdims=True))
    s = lax.dot_general(qb, kb, NT_DIMS, preferred_element_type=F32) * jnp.exp(d - m_t)
    inter = jnp.exp(g_col - m_t)
    cq = lax.dot_general(qb, c_prev.astype(BF16), NT_DIMS, preferred_element_type=F32)
    num = jnp.dot(s.astype(BF16), v.astype(BF16), preferred_element_type=F32) + inter * cq
    den = jnp.sum(s, axis=1, keepdims=True) + inter * jnp.sum(q * n_prev, axis=1, keepdims=True)
    hh = num / jnp.maximum(jnp.abs(den), jnp.exp(-m_t))

    m_new = m_t[chunk - 1:chunk, :]
    b_last = b_row[:, chunk - 1:chunk]
    w_col = jnp.exp(b_last - b_col + li_col - m_new)
    decay = jnp.exp(m_prev + b_last - m_new)
    vw = (v * w_col).T.astype(BF16)
    c_s[...] = decay * c_prev + jnp.dot(vw, kb, preferred_element_type=F32)
    n_s[...] = decay * n_prev + jnp.sum(w_col * k, axis=0, keepdims=True)
    m_s[...] = m_new

    mu = jnp.mean(hh, axis=1, keepdims=True)
    hc = hh - mu
    var = jnp.mean(hc * hc, axis=1, keepdims=True)
    hn = hc * lax.rsqrt(var + EPS) * nh_ref[...]
    y = jax.nn.sigmoid(o_ref[...].astype(F32)) * hn[:lb]
    y_ref[...] = y.astype(y_ref.dtype)

    @pl.when(ci == pl.num_programs(2) - 1)
    def _():
        cout_ref[...] = c_s[...]
        nout_ref[...] = n_s[...]
        mout_ref[...] = m_s[...]


def mlstm_mixer(p, gh, c0, n0, m0, norm_h, col0, chunk):
    b, t, _ = p.shape
    _, h, dv, dk = c0.shape
    lb = min(chunk, t)
    nc = pl.cdiv(t, chunk)
    blk0 = col0 // dk
    scale = float(dk) ** -0.5

    def pspec(group):
        return pl.BlockSpec((None, lb, dk), lambda i, hh, c: (i, c, blk0 + group * h + hh))

    state4 = lambda r, cdim: pl.BlockSpec((None, None, r, cdim), lambda i, hh, c: (i, hh, 0, 0))
    return pl.pallas_call(
        functools.partial(_mlstm_kernel, chunk=chunk, total=t, scale=scale),
        grid=(b, h, nc),
        in_specs=[pspec(0), pspec(1), pspec(2), pspec(3),
                  pl.BlockSpec((None, None, 2, chunk), lambda i, hh, c: (i, hh, 0, c)),
                  state4(dv, dk), state4(1, dk), state4(1, 1),
                  pl.BlockSpec((1, dv), lambda i, hh, c: (0, hh))],
        out_specs=[pl.BlockSpec((None, lb, dv), lambda i, hh, c: (i, c, hh)),
                   state4(dv, dk), state4(1, dk), state4(1, 1)],
        out_shape=[jax.ShapeDtypeStruct((b, t, h * dv), BF16),
                   jax.ShapeDtypeStruct((b, h, dv, dk), F32),
                   jax.ShapeDtypeStruct((b, h, 1, dk), F32),
                   jax.ShapeDtypeStruct((b, h, 1, 1), F32)],
        scratch_shapes=[pltpu.VMEM((dv, dk), F32), pltpu.VMEM((1, dk), F32), pltpu.VMEM((1, 1), F32)],
        compiler_params=_cparams("parallel", "parallel", "arbitrary"),
        name="mlstm_mixer",
    )(p, p, p, p, gh, c0, n0, m0, norm_h)


def _outproj_kernel(*refs, n_in):
    a_refs = refs[:n_in]
    w_ref, g_ref, x_ref, o_ref = refs[n_in:]
    y = None
    k0 = 0
    for a_ref in a_refs:
        kk = a_ref.shape[1]
        part = jnp.dot(a_ref[...], w_ref[k0:k0 + kk, :], preferred_element_type=F32)
        y = part if y is None else y + part
        k0 += kk
    o_ref[...] = x_ref[...] + _rms(y, g_ref[...])


def outproj_residual(a_list, w, layer, g, x, tile=688):
    b, t, d = x.shape
    tm = _row_tile(t, tile)
    kt = w.shape[1]
    a_specs = [pl.BlockSpec((None, tm, a.shape[2]), lambda i, j: (i, j, 0)) for a in a_list]
    return pl.pallas_call(
        functools.partial(_outproj_kernel, n_in=len(a_list)),
        grid=(b, pl.cdiv(t, tm)),
        in_specs=a_specs + [pl.BlockSpec((None, kt, d), lambda i, j: (layer, 0, 0)),
                            pl.BlockSpec((1, d), lambda i, j: (0, 0)),
                            pl.BlockSpec((None, tm, d), lambda i, j: (i, j, 0))],
        out_specs=pl.BlockSpec((None, tm, d), lambda i, j: (i, j, 0)),
        out_shape=jax.ShapeDtypeStruct((b, t, d), F32),
        compiler_params=_cparams("parallel", "parallel"),
        name="outproj_residual",
    )(*a_list, w, g, x)


def _mlp_kernel(x_ref, g1_ref, wu_ref, wd_ref, g2_ref, g3_ref, o_ref, on_ref, xn_s, acc_s):
    f = pl.program_id(2)

    @pl.when(f == 0)
    def _():
        xn_s[...] = _rms(x_ref[...], g1_ref[...]).astype(BF16)
        acc_s[...] = jnp.zeros_like(acc_s)

    h = jnp.maximum(jnp.dot(xn_s[...], wu_ref[...], preferred_element_type=F32), 0.0)
    acc_s[...] += jnp.dot((h * h).astype(BF16), wd_ref[...], preferred_element_type=F32)

    @pl.when(f == pl.num_programs(2) - 1)
    def _():
        xnew = x_ref[...] + _rms(acc_s[...], g2_ref[...])
        o_ref[...] = xnew
        on_ref[...] = _rms(xnew, g3_ref[...]).astype(on_ref.dtype)


def mlp_residual(x, g_pre, w_up, w_down, layer, g_post, g_next, tile=688, tf=512):
    b, t, d = x.shape
    f = w_up.shape[2]
    tm = _row_tile(t, tile)
    row = pl.BlockSpec((None, tm, d), lambda i, j, l: (i, j, 0))
    vec = pl.BlockSpec((1, d), lambda i, j, l: (0, 0))
    return pl.pallas_call(
        _mlp_kernel,
        grid=(b, pl.cdiv(t, tm), f // tf),
        in_specs=[row, vec,
                  pl.BlockSpec((None, d, tf), lambda i, j, l: (layer, 0, l)),
                  pl.BlockSpec((None, tf, d), lambda i, j, l: (layer, l, 0)),
                  vec, vec],
        out_specs=[row, row],
        out_shape=[jax.ShapeDtypeStruct((b, t, d), F32), jax.ShapeDtypeStruct((b, t, d), BF16)],
        scratch_shapes=[pltpu.VMEM((tm, d), BF16), pltpu.VMEM((tm, d), F32)],
        compiler_params=_cparams("parallel", "parallel", "arbitrary"),
        name="mlp_residual",
    )(x, g_pre, w_up, w_down, g_post, g_next)


def _fox_cumsum_kernel(lf_ref, crow_ref, ccol_ref, carry_ref):
    j = pl.program_id(1)

    @pl.when(j == 0)
    def _():
        carry_ref[...] = jnp.zeros_like(carry_ref)

    nh, tc = lf_ref.shape
    ri = lax.broadcasted_iota(jnp.int32, (tc, tc), 0)
    cj = lax.broadcasted_iota(jnp.int32, (tc, tc), 1)
    causal = cj <= ri
    eye = cj == ri
    lane = lax.broadcasted_iota(jnp.int32, (tc, LANES), 1)
    cols = jnp.zeros((tc, LANES), F32)
    for h in range(nh):
        col = jnp.sum(jnp.where(causal, lf_ref[h:h + 1, :], 0.0), axis=1, keepdims=True)
        col = col + carry_ref[h:h + 1, :]
        crow_ref[h:h + 1, :] = jnp.sum(jnp.where(eye, col, 0.0), axis=0, keepdims=True) * LOG2E
        carry_ref[h:h + 1, :] = col[tc - 1:tc, :]
        cols = jnp.where(lane == h, col, cols)
    ccol_ref[...] = cols * LOG2E


def fox_cumsum(lf, tile=512):
    b, nh, t = lf.shape
    tc = tile
    return pl.pallas_call(
        _fox_cumsum_kernel,
        grid=(b, pl.cdiv(t, tc)),
        in_specs=[pl.BlockSpec((None, nh, tc), lambda i, j: (i, 0, j))],
        out_specs=[pl.BlockSpec((None, nh, tc), lambda i, j: (i, 0, j)),
                   pl.BlockSpec((None, tc, LANES), lambda i, j: (i, j, 0))],
        out_shape=[jax.ShapeDtypeStruct((b, nh, t), F32), jax.ShapeDtypeStruct((b, t, LANES), F32)],
        scratch_shapes=[pltpu.VMEM((nh, 1), F32)],
        compiler_params=_cparams("parallel", "arbitrary"),
        name="fox_cumsum",
    )(lf)


def _fox_prompt_kernel(q_ref, k_ref, vt_ref, cq_ref, ck_ref, o_ref, m_s, l_s, acc_s, *, nh, hd, total, scale):
    qi = pl.program_id(1)
    ki = pl.program_id(2)
    tq = q_ref.shape[0]
    tk = k_ref.shape[0]

    @pl.when(ki == 0)
    def _():
        m_s[...] = jnp.full_like(m_s, NEG)
        l_s[...] = jnp.zeros_like(l_s)
        acc_s[...] = jnp.zeros_like(acc_s)

    def step(diagonal):
        if diagonal:
            kpos = ki * tk + lax.broadcasted_iota(jnp.int32, (tk, tq), 0)
            qpos = qi * tq + lax.broadcasted_iota(jnp.int32, (tk, tq), 1)
            visible = kpos <= qpos
            vcol_ok = (ki * tk + lax.broadcasted_iota(jnp.int32, (1, tk), 1)) < total
        for h in range(nh):
            sl = slice(h * hd, (h + 1) * hd)
            st = lax.dot_general(k_ref[:, sl].astype(BF16), q_ref[:, sl], NT_DIMS, preferred_element_type=F32)
            st = st * (scale * LOG2E) + cq_ref[h:h + 1, :] - ck_ref[:, h:h + 1]
            vt = vt_ref[sl, :]
            if diagonal:
                st = jnp.where(visible, st, NEG)
                vt = jnp.where(vcol_ok, vt, jnp.zeros_like(vt))
            m_prev = m_s[h:h + 1, :]
            m_new = jnp.maximum(m_prev, jnp.max(st, axis=0, keepdims=True))
            alpha = jnp.exp2(m_prev - m_new)
            p = jnp.exp2(st - m_new)
            l_s[h:h + 1, :] = alpha * l_s[h:h + 1, :] + jnp.sum(p, axis=0, keepdims=True)
            acc_s[sl, :] = alpha * acc_s[sl, :] + jnp.dot(vt, p.astype(BF16), preferred_element_type=F32)
            m_s[h:h + 1, :] = m_new

    @pl.when(ki < qi)
    def _():
        step(False)

    @pl.when(ki == qi)
    def _():
        step(True)
        for h in range(nh):
            sl = slice(h * hd, (h + 1) * hd)
            o_ref[:, sl] = (acc_s[sl, :] / l_s[h:h + 1, :]).T.astype(o_ref.dtype)


def fox_prompt_attention(q, k, vt, c_row, c_col, nh, tile=384):
    b, t, d = q.shape
    hd = d // nh
    tq = tk = tile
    nq = pl.cdiv(t, tq)
    kmap = lambda i, a, c: (i, jnp.minimum(c, a), 0)
    return pl.pallas_call(
        functools.partial(_fox_prompt_kernel, nh=nh, hd=hd, total=t, scale=float(hd) ** -0.5),
        grid=(b, nq, nq),
        in_specs=[pl.BlockSpec((None, tq, d), lambda i, a, c: (i, a, 0)),
                  pl.BlockSpec((None, tk, d), kmap),
                  pl.BlockSpec((None, d, tk), lambda i, a, c: (i, 0, jnp.minimum(c, a))),
                  pl.BlockSpec((None, nh, tq), lambda i, a, c: (i, 0, a)),
                  pl.BlockSpec((None, tk, LANES), kmap)],
        out_specs=pl.BlockSpec((None, tq, d), lambda i, a, c: (i, a, 0)),
        out_shape=jax.ShapeDtypeStruct((b, t, d), BF16),
        scratch_shapes=[pltpu.VMEM((nh, tq), F32), pltpu.VMEM((nh, tq), F32), pltpu.VMEM((d, tq), F32)],
        compiler_params=_cparams("parallel", "parallel", "arbitrary"),
        name="fox_prompt_attention",
    )(q, k, vt, c_row, c_col)


def _fox_sample_kernel(pt_ref, q_ref, kn_ref, vn_ref, lfn_ref, *refs, nh, hd, npp, scale):
    del pt_ref
    k_refs = refs[:npp]
    v_refs = refs[npp:2 * npp]
    lf_refs = refs[2 * npp:3 * npp]
    o_ref, qbdt_s, crow_s, m_s, l_s, acc_s, carry_s = refs[3 * npp:]
    j = pl.program_id(1)
    s_new = q_ref.shape[0]
    page = lf_refs[0].shape[1]
    nr = nh * s_new
    d = nh * hd
    npair = nh // 2

    expand_t = (lax.broadcasted_iota(jnp.int32, (nh, nr), 0)
                == _div_pow2(lax.broadcasted_iota(jnp.int32, (nh, nr), 1), s_new)).astype(F32)
    eye_r = lax.broadcasted_iota(jnp.int32, (nr, nr), 0) == lax.broadcasted_iota(jnp.int32, (nr, nr), 1)

    def head_rows(ref, h):
        return ref[pl.ds(h, page, stride=nh), :]

    def update(st, v_pairs):
        m_prev = m_s[...]
        m_new = jnp.maximum(m_prev, jnp.max(st, axis=0, keepdims=True))
        alpha = jnp.exp(m_prev - m_new)
        pt = jnp.exp(st - m_new)
        l_s[...] = alpha * l_s[...] + jnp.sum(pt, axis=0, keepdims=True)
        m_s[...] = m_new
        p = pt.T.astype(BF16)
        alpha_col = jnp.sum(jnp.where(eye_r, alpha, 0.0), axis=1, keepdims=True)
        for c in range(npair):
            rs = slice(2 * s_new * c, 2 * s_new * (c + 1))
            acc_s[rs, :] = alpha_col[rs, :] * acc_s[rs, :] + jnp.dot(p[rs, :], v_pairs[c], preferred_element_type=F32)

    @pl.when(j == 0)
    def _():
        qrep = jnp.concatenate([q_ref[...].astype(F32)] * nh, axis=0)
        rh = _div_pow2(lax.broadcasted_iota(jnp.int32, (nr, d), 0), s_new)
        ch = _div_pow2(lax.broadcasted_iota(jnp.int32, (nr, d), 1), hd)
        qbdt_s[...] = jnp.where(rh == ch, qrep, 0.0).T.astype(BF16)
        carry_s[...] = jnp.zeros_like(carry_s)
        m_s[...] = jnp.full_like(m_s, NEG)
        l_s[...] = jnp.zeros_like(l_s)
        acc_s[...] = jnp.zeros_like(acc_s)

        si = lax.broadcasted_iota(jnp.int32, (LANES, LANES), 0)
        sj = lax.broadcasted_iota(jnp.int32, (LANES, LANES), 1)
        lower = (sj <= si).astype(F32)
        c_new_t = lax.dot_general(lower, lfn_ref[...], NT_DIMS, preferred_element_type=F32, precision=HIGHEST)
        c_b = jnp.dot(c_new_t, expand_t, preferred_element_type=F32, precision=HIGHEST)
        key = lax.broadcasted_iota(jnp.int32, (LANES, nr), 0)
        tok = _mod_pow2(lax.broadcasted_iota(jnp.int32, (LANES, nr), 1), s_new)
        crow = jnp.sum(jnp.where(key == tok, c_b, 0.0), axis=0, keepdims=True)
        crow_s[...] = crow
        pad = jnp.zeros((LANES - s_new, d), F32)
        kn = jnp.concatenate([kn_ref[...], pad], axis=0).astype(BF16)
        vn = jnp.concatenate([vn_ref[...], pad], axis=0).astype(BF16)
        st = jnp.dot(kn, qbdt_s[...], preferred_element_type=F32) * scale
        st = jnp.where((key < s_new) & (key <= tok), st + crow - c_b, NEG)
        update(st, [vn[:, 2 * hd * c:2 * hd * (c + 1)] for c in range(npair)])

    pi = lax.broadcasted_iota(jnp.int32, (page, page), 0)
    pj = lax.broadcasted_iota(jnp.int32, (page, page), 1)
    later = (pj > pi).astype(F32)
    ones = jnp.ones((8, page), F32)
    st_parts = []
    for i in range(npp):
        lfp = lf_refs[i][...]
        rev = lax.dot_general(later, lfp, NT_DIMS, preferred_element_type=F32, precision=HIGHEST) + carry_s[...]
        carry_s[...] = carry_s[...] + lax.dot_general(ones, lfp, NT_DIMS, preferred_element_type=F32,
                                                      precision=HIGHEST)[0:1, :]
        bias = jnp.dot(rev, expand_t, preferred_element_type=F32, precision=HIGHEST)
        kb = jnp.concatenate([head_rows(k_refs[i], h).astype(BF16) for h in range(nh)], axis=1)
        st = jnp.dot(kb, qbdt_s[...], preferred_element_type=F32) * scale
        st_parts.append(st + crow_s[...] + bias)
    v_pairs = []
    for c in range(npair):
        v_pairs.append(jnp.concatenate(
            [jnp.concatenate([head_rows(v_refs[i], 2 * c).astype(BF16),
                              head_rows(v_refs[i], 2 * c + 1).astype(BF16)], axis=1) for i in range(npp)], axis=0))
    update(jnp.concatenate(st_parts, axis=0), v_pairs)

    @pl.when(j == pl.num_programs(1) - 1)
    def _():
        l_col = jnp.sum(jnp.where(eye_r, l_s[...], 0.0), axis=1, keepdims=True)
        for h in range(nh):
            rs = slice(h * s_new, (h + 1) * s_new)
            cs = slice((h % 2) * hd, (h % 2 + 1) * hd)
            o_ref[:, h * hd:(h + 1) * hd] = (acc_s[rs, cs] / l_col[rs, :]).astype(o_ref.dtype)


def fox_sample_attention(q, k_new, v_new, lf_new, cache_k, cache_v, cache_lf, page_table, layer, nh, npp=4):
    bd, s_new, d = q.shape
    hd = d // nh
    page = cache_lf.shape[3]
    npages = page_table.shape[1]
    assert npages % npp == 0 and nh * s_new == LANES and page == LANES and nh % 2 == 0
    nr = nh * s_new

    def paged(rows, width, i):
        return pl.BlockSpec((None, None, rows, width),
                            lambda b, j, pt: (layer, pt[b, npages - 1 - (j * npp + i)], 0, 0))

    seq = lambda r, w: pl.BlockSpec((None, r, w), lambda b, j, pt: (b, 0, 0))
    grid_spec = pltpu.PrefetchScalarGridSpec(
        num_scalar_prefetch=1,
        grid=(bd, npages // npp),
        in_specs=[seq(s_new, d), seq(s_new, d), seq(s_new, d), seq(nh, LANES)]
                 + [paged(page * nh, hd, i) for i in range(npp)]
                 + [paged(page * nh, hd, i) for i in range(npp)]
                 + [paged(nh, page, i) for i in range(npp)],
        out_specs=seq(s_new, d),
        scratch_shapes=[pltpu.VMEM((d, nr), BF16), pltpu.VMEM((1, nr), F32), pltpu.VMEM((1, nr), F32),
                        pltpu.VMEM((1, nr), F32), pltpu.VMEM((nr, 2 * hd), F32), pltpu.VMEM((1, nh), F32)],
    )
    return pl.pallas_call(
        functools.partial(_fox_sample_kernel, nh=nh, hd=hd, npp=npp, scale=float(hd) ** -0.5),
        grid_spec=grid_spec,
        out_shape=jax.ShapeDtypeStruct((bd, s_new, d), BF16),
        compiler_params=_cparams("parallel", "arbitrary"),
        name="fox_sample_attention",
    )(page_table, q, k_new, v_new, lf_new, *([cache_k] * npp), *([cache_v] * npp), *([cache_lf] * npp))


def kernel(x_prompt, x_sample, state_conv, state_C, state_n, state_m, cache_k, cache_v, cache_logf,
           page_table, meta_tokens, norm_mix_pre, norm_mix_post, norm_mlp_pre, norm_mlp_post,
           w_in_ab, conv_w, b_igate, b_fgate, mlstm_norm, w_out_ab, w_in_c, b_fox, w_out_c,
           w_mlp_up, w_mlp_down):
    bp, seq, d_model = x_prompt.shape
    bs, s_new, _ = x_sample.shape
    depth = norm_mix_pre.shape[0]
    n_meta = meta_tokens.shape[0]
    _, _, mh, hdb, _ = state_C.shape
    d_mlstm = mh * hdb
    d_conv = conv_w.shape[2]
    n_odd, n_pool, page, fh, fhd = cache_k.shape
    d_fox = fh * fhd
    tp = n_meta + seq

    meta = jnp.broadcast_to(meta_tokens[None], (bp, n_meta, d_model))
    xp = jnp.concatenate([meta, x_prompt], axis=1)
    xs = x_sample.reshape(1, bs * s_new, d_model)
    vec = lambda a: a.reshape(1, -1)

    ck = cache_k.reshape(n_odd, n_pool, page * fh, fhd)
    cv = cache_v.reshape(n_odd, n_pool, page * fh, fhd)
    clf = jnp.swapaxes(cache_logf, 2, 3)

    wt_ab = jnp.swapaxes(w_in_ab, 1, 2)
    wt_c = jnp.swapaxes(w_in_c, 1, 2)
    w_out_ab_b = w_out_ab.astype(BF16)
    w_out_c_b = w_out_c.astype(BF16)
    w_up_b = w_mlp_up.astype(BF16)
    w_down_b = w_mlp_down.astype(BF16)

    xpn = rmsnorm(xp, vec(norm_mix_pre[0]))
    xsn = rmsnorm(xs, vec(norm_mix_pre[0]))

    conv_p, c_p, n_p, m_p, conv_s, c_s, n_s, m_s = [], [], [], [], [], [], [], []
    k_p, v_p, lf_p, k_s, v_s, lf_s = [], [], [], [], [], []
    for layer in range(depth):
        if layer % 2 == 0:
            e = layer // 2
            n_main = 3 * d_conv + 4 * d_mlstm
            wg_t = wt_ab[e, n_main:, :]
            b_g = jnp.concatenate([b_igate[e], b_fgate[e]]).reshape(2 * mh, 1)
            nh_vec = vec(mlstm_norm[e])

            def mix(xn, prev, c0, n0, m0, chunk, nb):
                t = xn.shape[1]
                proj = matmul_cols(xn, wt_ab, e, 0, n_main, BF16)
                g = gates(xn, wg_t, b_g, mh)
                proj = proj.reshape(nb, t // (nb // xn.shape[0]), n_main)
                tl = proj.shape[1]
                g = g.reshape(xn.shape[0], 2, mh, nb // xn.shape[0], tl)
                gh = jnp.transpose(g, (0, 3, 2, 1, 4)).reshape(nb, mh, 2, tl)
                if tl < chunk:
                    gh = jnp.pad(gh, ((0, 0), (0, 0), (0, 0), (0, chunk - tl)))
                ya, cnew = conv_mixer(proj, prev, conv_w[e])
                yb, c1, n1, m1 = mlstm_mixer(proj, gh, c0, n0.reshape(nb, mh, 1, hdb), m0.reshape(nb, mh, 1, 1),
                                             nh_vec, 3 * d_conv, chunk)
                shp = (xn.shape[0], t, -1)
                return ya.reshape(shp), yb.reshape(shp), cnew, c1, n1.reshape(nb, mh, hdb), m1.reshape(nb, mh)

            zc = jnp.zeros((bp, mh, hdb, hdb), F32)
            ya, yb, cvp, c1, n1, m1 = mix(xpn, jnp.zeros((bp, 2, d_conv), F32), zc,
                                          jnp.zeros((bp, mh, hdb), F32), jnp.full((bp, mh), M_INIT, F32), 256, bp)
            conv_p.append(cvp); c_p.append(c1); n_p.append(n1); m_p.append(m1)
            xp = outproj_residual([ya, yb], w_out_ab_b, e, vec(norm_mix_post[layer]), xp)
            ya, yb, cvs, c1, n1, m1 = mix(xsn, state_conv[e], state_C[e], state_n[e], state_m[e], LANES, bs)
            conv_s.append(cvs); c_s.append(c1); n_s.append(n1); m_s.append(m1)
            xs = outproj_residual([ya, yb], w_out_ab_b, e, vec(norm_mix_post[layer]), xs)
        else:
            o = layer // 2
            wg_t = wt_c[o, 3 * d_fox:, :]
            b_g = b_fox[o].reshape(fh, 1)

            q = matmul_cols(xpn, wt_c, o, 0, d_fox, BF16)
            k = matmul_cols(xpn, wt_c, o, d_fox, d_fox, F32)
            v, vt = matmul_cols(xpn, wt_c, o, 2 * d_fox, d_fox, F32, tile=1408, transposed_copy=True)
            lf = gates(xpn, wg_t, b_g, 0)
            c_row, c_col = fox_cumsum(lf)
            att = fox_prompt_attention(q, k, vt, c_row, c_col, fh)
            k_p.append(k.reshape(bp, tp, fh, fhd)); v_p.append(v.reshape(bp, tp, fh, fhd))
            lf_p.append(jnp.transpose(lf, (0, 2, 1)))
            xp = outproj_residual([att], w_out_c_b, o, vec(norm_mix_post[layer]), xp)

            q = matmul_cols(xsn, wt_c, o, 0, d_fox, BF16).reshape(bs, s_new, d_fox)
            k = matmul_cols(xsn, wt_c, o, d_fox, d_fox, F32).reshape(bs, s_new, d_fox)
            v = matmul_cols(xsn, wt_c, o, 2 * d_fox, d_fox, F32).reshape(bs, s_new, d_fox)
            lf = gates(xsn, wg_t, b_g, 0)
            lf = jnp.transpose(lf.reshape(fh, bs, s_new), (1, 0, 2))
            lf_pad = jnp.pad(lf, ((0, 0), (0, 0), (0, LANES - s_new)))
            att = fox_sample_attention(q, k, v, lf_pad, ck, cv, clf, page_table, o, fh)
            k_s.append(k.reshape(bs, s_new, fh, fhd)); v_s.append(v.reshape(bs, s_new, fh, fhd))
            lf_s.append(jnp.transpose(lf, (0, 2, 1)))
            xs = outproj_residual([att.reshape(1, bs * s_new, d_fox)], w_out_c_b, o, vec(norm_mix_post[layer]), xs)

        g_next = vec(norm_mix_pre[layer + 1]) if layer + 1 < depth else vec(norm_mix_pre[layer])
        xp, xpn = mlp_residual(xp, vec(norm_mlp_pre[layer]), w_up_b, w_down_b, layer, vec(norm_mlp_post[layer]), g_next)
        xs, xsn = mlp_residual(xs, vec(norm_mlp_pre[layer]), w_up_b, w_down_b, layer, vec(norm_mlp_post[layer]), g_next)

    y_prompt = xp[:, n_meta:]
    y_sample = xs.reshape(bs, s_new, d_model)
    return (y_prompt, y_sample,
            jnp.stack(conv_p), jnp.stack(c_p), jnp.stack(n_p), jnp.stack(m_p),
            jnp.stack(k_p), jnp.stack(v_p), jnp.stack(lf_p),
            jnp.stack(conv_s), jnp.stack(c_s), jnp.stack(n_s), jnp.stack(m_s),
            jnp.stack(k_s), jnp.stack(v_s), jnp.stack(lf_s))
```

```python
import functools

import jax
import jax.numpy as jnp
from jax import lax
from jax.experimental import pallas as pl
from jax.experimental.pallas import tpu as pltpu

F32 = jnp.float32
BF16 = jnp.bfloat16
EPS = 1e-6
M_INIT = -1e30
NEG = -1e30
HIGHEST = lax.Precision.HIGHEST
LOG2E = 1.4426950408889634

V7X_VMEM_BYTES = 64 * 1024 * 1024
VMEM_LIMIT = V7X_VMEM_BYTES - 8 * 1024 * 1024
LANES = 128
BF16_SUBLANES = 16

NT_DIMS = (((1,), (1,)), ((), ()))


def _cparams(*sem):
    return pltpu.CompilerParams(dimension_semantics=sem, vmem_limit_bytes=VMEM_LIMIT)


def _round_up(x, m):
    return -(-x // m) * m


def _row_tile(t, target, align=BF16_SUBLANES):
    if t <= target:
        return t
    n = pl.cdiv(t, target)
    return _round_up(pl.cdiv(t, n), align)


def _rms(x, g):
    ms = jnp.mean(x * x, axis=-1, keepdims=True)
    return x * lax.rsqrt(ms + EPS) * g


def _div_pow2(x, n):
    assert n & (n - 1) == 0
    return lax.shift_right_logical(x, n.bit_length() - 1)


def _mod_pow2(x, n):
    assert n & (n - 1) == 0
    return lax.bitwise_and(x, n - 1)


def _log_sigmoid(x):
    return jnp.minimum(x, 0.0) - jnp.log1p(jnp.exp(-jnp.abs(x)))


def _rmsnorm_kernel(x_ref, g_ref, o_ref):
    o_ref[...] = _rms(x_ref[...], g_ref[...]).astype(o_ref.dtype)


def rmsnorm(x, g, tile=1024):
    b, t, d = x.shape
    tm = _row_tile(t, tile)
    return pl.pallas_call(
        _rmsnorm_kernel,
        grid=(b, pl.cdiv(t, tm)),
        in_specs=[pl.BlockSpec((None, tm, d), lambda i, j: (i, j, 0)),
                  pl.BlockSpec((1, d), lambda i, j: (0, 0))],
        out_specs=pl.BlockSpec((None, tm, d), lambda i, j: (i, j, 0)),
        out_shape=jax.ShapeDtypeStruct((b, t, d), BF16),
        compiler_params=_cparams("parallel", "parallel"),
        name="rmsnorm",
    )(x, g)


def _matmul_kernel(x_ref, wt_ref, o_ref, *rest):
    wb_ref = rest[-1]

    @pl.when((pl.program_id(1) == 0) & (pl.program_id(2) == 0))
    def _():
        wb_ref[...] = wt_ref[...].astype(BF16)

    y = lax.dot_general(x_ref[...], wb_ref[...], NT_DIMS, preferred_element_type=F32)
    o_ref[...] = y.astype(o_ref.dtype)
    if len(rest) == 2:
        rest[0][...] = y.T.astype(rest[0].dtype)


def matmul_cols(x, wt, layer, col0, ncols, out_dtype, tn=1024, tile=1376, transposed_copy=False):
    b, t, k = x.shape
    tm = _row_tile(t, tile, LANES if transposed_copy else BF16_SUBLANES)
    assert col0 % tn == 0 and ncols % tn == 0
    c0 = col0 // tn
    out_specs = [pl.BlockSpec((None, tm, tn), lambda n, i, j: (i, j, n))]
    out_shape = [jax.ShapeDtypeStruct((b, t, ncols), out_dtype)]
    if transposed_copy:
        assert tm % LANES == 0
        out_specs.append(pl.BlockSpec((None, tn, tm), lambda n, i, j: (i, n, j)))
        out_shape.append(jax.ShapeDtypeStruct((b, ncols, t), BF16))
    res = pl.pallas_call(
        _matmul_kernel,
        grid=(ncols // tn, b, pl.cdiv(t, tm)),
        in_specs=[pl.BlockSpec((None, tm, k), lambda n, i, j: (i, j, 0)),
                  pl.BlockSpec((None, tn, k), lambda n, i, j: (layer, c0 + n, 0))],
        out_specs=out_specs,
        out_shape=out_shape,
        scratch_shapes=[pltpu.VMEM((tn, k), BF16)],
        compiler_params=_cparams("arbitrary", "arbitrary", "arbitrary"),
        name="matmul_cols",
    )(x, wt)
    return res if transposed_copy else res[0]


def _gates_kernel(x_ref, wt_ref, b_ref, o_ref, *, n_plain):
    g = lax.dot_general(wt_ref[...].astype(BF16), x_ref[...], NT_DIMS, preferred_element_type=F32)
    g = g + b_ref[...]
    row = lax.broadcasted_iota(jnp.int32, g.shape, 0)
    o_ref[...] = jnp.where(row < n_plain, g, _log_sigmoid(g))


def gates(x, wt, bias, n_plain, tile=1024):
    b, t, k = x.shape
    ng = wt.shape[0]
    tm = t if t <= tile else tile
    return pl.pallas_call(
        functools.partial(_gates_kernel, n_plain=n_plain),
        grid=(b, pl.cdiv(t, tm)),
        in_specs=[pl.BlockSpec((None, tm, k), lambda i, j: (i, j, 0)),
                  pl.BlockSpec((ng, k), lambda i, j: (0, 0)),
                  pl.BlockSpec((ng, 1), lambda i, j: (0, 0))],
        out_specs=pl.BlockSpec((None, ng, tm), lambda i, j: (i, 0, j)),
        out_shape=jax.ShapeDtypeStruct((b, ng, t), F32),
        compiler_params=_cparams("parallel", "parallel"),
        name="gates",
    )(x, wt, bias)


def _conv_kernel(ab_ref, ac_ref, ax_ref, prev_ref, w_ref, y_ref, new_ref, carry_ref, *, tail):
    j = pl.program_id(1)

    @pl.when(j == 0)
    def _():
        carry_ref[...] = prev_ref[...]

    u = ac_ref[...].astype(F32) * ax_ref[...].astype(F32)
    tc = u.shape[0]
    row = lax.broadcasted_iota(jnp.int32, u.shape, 0)
    c0 = carry_ref[0:1, :]
    c1 = carry_ref[1:2, :]
    u1 = jnp.where(row == 0, c1, pltpu.roll(u, 1, axis=0))
    u2 = jnp.where(row == 0, c0, jnp.where(row == 1, c1, pltpu.roll(u, 2, axis=0)))
    z = w_ref[0:1, :] * u2 + w_ref[1:2, :] * u1 + w_ref[2:3, :] * u
    y_ref[...] = (ab_ref[...].astype(F32) * z).astype(y_ref.dtype)
    carry_ref[...] = u[tc - 2:tc, :]

    @pl.when(j == pl.num_programs(1) - 1)
    def _():
        new_ref[...] = u[tail:tail + 2, :]


def conv_mixer(p, prev, w, tile=688):
    b, t, _ = p.shape
    c = w.shape[1]
    tc = _row_tile(t, tile)
    nt = pl.cdiv(t, tc)
    tail = (t - 2) - (nt - 1) * tc
    assert 0 <= tail and tail + 2 <= tc
    return pl.pallas_call(
        functools.partial(_conv_kernel, tail=tail),
        grid=(b, nt),
        in_specs=[pl.BlockSpec((None, tc, c), lambda i, j: (i, j, 0)),
                  pl.BlockSpec((None, tc, c), lambda i, j: (i, j, 1)),
                  pl.BlockSpec((None, tc, c), lambda i, j: (i, j, 2)),
                  pl.BlockSpec((None, 2, c), lambda i, j: (i, 0, 0)),
                  pl.BlockSpec((3, c), lambda i, j: (0, 0))],
        out_specs=[pl.BlockSpec((None, tc, c), lambda i, j: (i, j, 0)),
                   pl.BlockSpec((None, 2, c), lambda i, j: (i, 0, 0))],
        out_shape=[jax.ShapeDtypeStruct((b, t, c), BF16),
                   jax.ShapeDtypeStruct((b, 2, c), F32)],
        scratch_shapes=[pltpu.VMEM((2, c), F32)],
        compiler_params=_cparams("parallel", "arbitrary"),
        name="conv_mixer",
    )(p, p, p, prev, w)


def _mlstm_kernel(q_ref, k_ref, v_ref, o_ref, g_ref, c0_ref, n0_ref, m0_ref, nh_ref,
                  y_ref, cout_ref, nout_ref, mout_ref, c_s, n_s, m_s, *, chunk, total, scale):
    ci = pl.program_id(2)

    @pl.when(ci == 0)
    def _():
        c_s[...] = c0_ref[...]
        n_s[...] = n0_ref[...]
        m_s[...] = m0_ref[...]

    lb = q_ref.shape[0]
    valid = total - ci * chunk
    rows = lax.broadcasted_iota(jnp.int32, (chunk, 1), 0)

    def load(ref):
        x = ref[...].astype(F32)
        if lb < chunk:
            x = jnp.concatenate([x, jnp.zeros((chunk - lb, x.shape[1]), F32)], axis=0)
        return jnp.where(rows < valid, x, 0.0)

    q = load(q_ref)
    k = load(k_ref) * scale
    v = load(v_ref)
    qb = q.astype(BF16)
    kb = k.astype(BF16)

    lane = lax.broadcasted_iota(jnp.int32, (1, chunk), 1)
    li = jnp.where(lane < valid, g_ref[0:1, :], NEG)
    lf = jnp.where(lane < valid, g_ref[1:2, :], 0.0)

    ri = lax.broadcasted_iota(jnp.int32, (chunk, chunk), 0)
    cj = lax.broadcasted_iota(jnp.int32, (chunk, chunk), 1)
    causal = cj <= ri
    eye = cj == ri
    b_col = jnp.sum(jnp.where(causal, lf, 0.0), axis=1, keepdims=True)
    b_row = jnp.sum(jnp.where(eye, b_col, 0.0), axis=0, keepdims=True)
    li_col = jnp.sum(jnp.where(eye, li, 0.0), axis=1, keepdims=True)

    m_prev = m_s[...]
    c_prev = c_s[...]
    n_prev = n_s[...]
    d = jnp.where(causal, b_col - b_row + li, NEG)
    g_col = m_prev + b_col
    m_t = jnp.maximum(g_col, jnp.max(d, axis=1, keepdims=True))
    s = lax.dot_general(qb, kb, NT_DIMS, preferred_element_type=F32) * jnp.exp(d - m_t)
    inter = jnp.exp(g_col - m_t)
    cq = lax.dot_general(qb, c_prev.astype(BF16), NT_DIMS, preferred_element_type=F32)
    num = jnp.dot(s.astype(BF16), v.astype(BF16), preferred_element_type=F32) + inter * cq
    den = jnp.sum(s, axis=1, keepdims=True) + inter * jnp.sum(q * n_prev, axis=1, keepdims=True)
    hh = num / jnp.maximum(jnp.abs(den), jnp.exp(-m_t))

    m_new = m_t[chunk - 1:chunk, :]
    b_last = b_row[:, chunk - 1:chunk]
    w_col = jnp.exp(b_last - b_col + li_col - m_new)
    decay = jnp.exp(m_prev + b_last - m_new)
    vw = (v * w_col).T.astype(BF16)
    c_s[...] = decay * c_prev + jnp.dot(vw, kb, preferred_element_type=F32)
    n_s[...] = decay * n_prev + jnp.sum(w_col * k, axis=0, keepdims=True)
    m_s[...] = m_new

    mu = jnp.mean(hh, axis=1, keepdims=True)
    hc = hh - mu
    var = jnp.mean(hc * hc, axis=1, keepdims=True)
    hn = hc * lax.rsqrt(var + EPS) * nh_ref[...]
    y = jax.nn.sigmoid(o_ref[...].astype(F32)) * hn[:lb]
    y_ref[...] = y.astype(y_ref.dtype)

    @pl.when(ci == pl.num_programs(2) - 1)
    def _():
        cout_ref[...] = c_s[...]
        nout_ref[...] = n_s[...]
        mout_ref[...] = m_s[...]


def mlstm_mixer(p, gh, c0, n0, m0, norm_h, col0, chunk):
    b, t, _ = p.shape
    _, h, dv, dk = c0.shape
    lb = min(chunk, t)
    nc = pl.cdiv(t, chunk)
    blk0 = col0 // dk
    scale = float(dk) ** -0.5

    def pspec(group):
        return pl.BlockSpec((None, lb, dk), lambda i, hh, c: (i, c, blk0 + group * h + hh))

    state4 = lambda r, cdim: pl.BlockSpec((None, None, r, cdim), lambda i, hh, c: (i, hh, 0, 0))
    return pl.pallas_call(
        functools.partial(_mlstm_kernel, chunk=chunk, total=t, scale=scale),
        grid=(b, h, nc),
        in_specs=[pspec(0), pspec(1), pspec(2), pspec(3),
                  pl.BlockSpec((None, None, 2, chunk), lambda i, hh, c: (i, hh, 0, c)),
                  state4(dv, dk), state4(1, dk), state4(1, 1),
                  pl.BlockSpec((1, dv), lambda i, hh, c: (0, hh))],
        out_specs=[pl.BlockSpec((None, lb, dv), lambda i, hh, c: (i, c, hh)),
                   state4(dv, dk), state4(1, dk), state4(1, 1)],
        out_shape=[jax.ShapeDtypeStruct((b, t, h * dv), BF16),
                   jax.ShapeDtypeStruct((b, h, dv, dk), F32),
                   jax.ShapeDtypeStruct((b, h, 1, dk), F32),
                   jax.ShapeDtypeStruct((b, h, 1, 1), F32)],
        scratch_shapes=[pltpu.VMEM((dv, dk), F32), pltpu.VMEM((1, dk), F32), pltpu.VMEM((1, 1), F32)],
        compiler_params=_cparams("parallel", "parallel", "arbitrary"),
        name="mlstm_mixer",
    )(p, p, p, p, gh, c0, n0, m0, norm_h)


def _outproj_kernel(*refs, n_in):
    a_refs = refs[:n_in]
    w_ref, g_ref, x_ref, o_ref = refs[n_in:]
    y = None
    k0 = 0
    for a_ref in a_refs:
        kk = a_ref.shape[1]
        part = jnp.dot(a_ref[...], w_ref[k0:k0 + kk, :], preferred_element_type=F32)
        y = part if y is None else y + part
        k0 += kk
    o_ref[...] = x_ref[...] + _rms(y, g_ref[...])


def outproj_residual(a_list, w, layer, g, x, tile=688):
    b, t, d = x.shape
    tm = _row_tile(t, tile)
    kt = w.shape[1]
    a_specs = [pl.BlockSpec((None, tm, a.shape[2]), lambda i, j: (i, j, 0)) for a in a_list]
    return pl.pallas_call(
        functools.partial(_outproj_kernel, n_in=len(a_list)),
        grid=(b, pl.cdiv(t, tm)),
        in_specs=a_specs + [pl.BlockSpec((None, kt, d), lambda i, j: (layer, 0, 0)),
                            pl.BlockSpec((1, d), lambda i, j: (0, 0)),
                            pl.BlockSpec((None, tm, d), lambda i, j: (i, j, 0))],
        out_specs=pl.BlockSpec((None, tm, d), lambda i, j: (i, j, 0)),
        out_shape=jax.ShapeDtypeStruct((b, t, d), F32),
        compiler_params=_cparams("parallel", "parallel"),
        name="outproj_residual",
    )(*a_list, w, g, x)


def _mlp_kernel(x_ref, g1_ref, wu_ref, wd_ref, g2_ref, g3_ref, o_ref, on_ref, xn_s, acc_s):
    f = pl.program_id(2)

    @pl.when(f == 0)
    def _():
        xn_s[...] = _rms(x_ref[...], g1_ref[...]).astype(BF16)
        acc_s[...] = jnp.zeros_like(acc_s)

    h = jnp.maximum(jnp.dot(xn_s[...], wu_ref[...], preferred_element_type=F32), 0.0)
    acc_s[...] += jnp.dot((h * h).astype(BF16), wd_ref[...], preferred_element_type=F32)

    @pl.when(f == pl.num_programs(2) - 1)
    def _():
        xnew = x_ref[...] + _rms(acc_s[...], g2_ref[...])
        o_ref[...] = xnew
        on_ref[...] = _rms(xnew, g3_ref[...]).astype(on_ref.dtype)


def mlp_residual(x, g_pre, w_up, w_down, layer, g_post, g_next, tile=688, tf=512):
    b, t, d = x.shape
    f = w_up.shape[2]
    tm = _row_tile(t, tile)
    row = pl.BlockSpec((None, tm, d), lambda i, j, l: (i, j, 0))
    vec = pl.BlockSpec((1, d), lambda i, j, l: (0, 0))
    return pl.pallas_call(
        _mlp_kernel,
        grid=(b, pl.cdiv(t, tm), f // tf),
        in_specs=[row, vec,
                  pl.BlockSpec((None, d, tf), lambda i, j, l: (layer, 0, l)),
                  pl.BlockSpec((None, tf, d), lambda i, j, l: (layer, l, 0)),
                  vec, vec],
        out_specs=[row, row],
        out_shape=[jax.ShapeDtypeStruct((b, t, d), F32), jax.ShapeDtypeStruct((b, t, d), BF16)],
        scratch_shapes=[pltpu.VMEM((tm, d), BF16), pltpu.VMEM((tm, d), F32)],
        compiler_params=_cparams("parallel", "parallel", "arbitrary"),
        name="mlp_residual",
    )(x, g_pre, w_up, w_down, g_post, g_next)


def _fox_cumsum_kernel(lf_ref, crow_ref, ccol_ref, carry_ref):
    j = pl.program_id(1)

    @pl.when(j == 0)
    def _():
        carry_ref[...] = jnp.zeros_like(carry_ref)

    nh, tc = lf_ref.shape
    ri = lax.broadcasted_iota(jnp.int32, (tc, tc), 0)
    cj = lax.broadcasted_iota(jnp.int32, (tc, tc), 1)
    causal = cj <= ri
    eye = cj == ri
    lane = lax.broadcasted_iota(jnp.int32, (tc, LANES), 1)
    cols = jnp.zeros((tc, LANES), F32)
    for h in range(nh):
        col = jnp.sum(jnp.where(causal, lf_ref[h:h + 1, :], 0.0), axis=1, keepdims=True)
        col = col + carry_ref[h:h + 1, :]
        crow_ref[h:h + 1, :] = jnp.sum(jnp.where(eye, col, 0.0), axis=0, keepdims=True) * LOG2E
        carry_ref[h:h + 1, :] = col[tc - 1:tc, :]
        cols = jnp.where(lane == h, col, cols)
    ccol_ref[...] = cols * LOG2E


def fox_cumsum(lf, tile=512):
    b, nh, t = lf.shape
    tc = tile
    return pl.pallas_call(
        _fox_cumsum_kernel,
        grid=(b, pl.cdiv(t, tc)),
        in_specs=[pl.BlockSpec((None, nh, tc), lambda i, j: (i, 0, j))],
        out_specs=[pl.BlockSpec((None, nh, tc), lambda i, j: (i, 0, j)),
                   pl.BlockSpec((None, tc, LANES), lambda i, j: (i, j, 0))],
        out_shape=[jax.ShapeDtypeStruct((b, nh, t), F32), jax.ShapeDtypeStruct((b, t, LANES), F32)],
        scratch_shapes=[pltpu.VMEM((nh, 1), F32)],
        compiler_params=_cparams("parallel", "arbitrary"),
        name="fox_cumsum",
    )(lf)


def _fox_prompt_kernel(q_ref, k_ref, vt_ref, cq_ref, ck_ref, o_ref, m_s, l_s, acc_s, *, nh, hd, total, scale):
    qi = pl.program_id(1)
    ki = pl.program_id(2)
    tq = q_ref.shape[0]
    tk = k_ref.shape[0]

    @pl.when(ki == 0)
    def _():
        m_s[...] = jnp.full_like(m_s, NEG)
        l_s[...] = jnp.zeros_like(l_s)
        acc_s[...] = jnp.zeros_like(acc_s)

    def step(diagonal):
        if diagonal:
            kpos = ki * tk + lax.broadcasted_iota(jnp.int32, (tk, tq), 0)
            qpos = qi * tq + lax.broadcasted_iota(jnp.int32, (tk, tq), 1)
            visible = kpos <= qpos
            vcol_ok = (ki * tk + lax.broadcasted_iota(jnp.int32, (1, tk), 1)) < total
        for h in range(nh):
            sl = slice(h * hd, (h + 1) * hd)
            st = lax.dot_general(k_ref[:, sl].astype(BF16), q_ref[:, sl], NT_DIMS, preferred_element_type=F32)
            st = st * (scale * LOG2E) + cq_ref[h:h + 1, :] - ck_ref[:, h:h + 1]
            vt = vt_ref[sl, :]
            if diagonal:
                st = jnp.where(visible, st, NEG)
                vt = jnp.where(vcol_ok, vt, jnp.zeros_like(vt))
            m_prev = m_s[h:h + 1, :]
            m_new = jnp.maximum(m_prev, jnp.max(st, axis=0, keepdims=True))
            alpha = jnp.exp2(m_prev - m_new)
            p = jnp.exp2(st - m_new)
            l_s[h:h + 1, :] = alpha * l_s[h:h + 1, :] + jnp.sum(p, axis=0, keepdims=True)
            acc_s[sl, :] = alpha * acc_s[sl, :] + jnp.dot(vt, p.astype(BF16), preferred_element_type=F32)
            m_s[h:h + 1, :] = m_new

    @pl.when(ki < qi)
    def _():
        step(False)

    @pl.when(ki == qi)
    def _():
        step(True)
        for h in range(nh):
            sl = slice(h * hd, (h + 1) * hd)
            o_ref[:, sl] = (acc_s[sl, :] / l_s[h:h + 1, :]).T.astype(o_ref.dtype)


def fox_prompt_attention(q, k, vt, c_row, c_col, nh, tile=512):
    b, t, d = q.shape
    hd = d // nh
    tq = tk = tile
    nq = pl.cdiv(t, tq)
    kmap = lambda i, a, c: (i, jnp.minimum(c, a), 0)
    return pl.pallas_call(
        functools.partial(_fox_prompt_kernel, nh=nh, hd=hd, total=t, scale=float(hd) ** -0.5),
        grid=(b, nq, nq),
        in_specs=[pl.BlockSpec((None, tq, d), lambda i, a, c: (i, a, 0)),
                  pl.BlockSpec((None, tk, d), kmap),
                  pl.BlockSpec((None, d, tk), lambda i, a, c: (i, 0, jnp.minimum(c, a))),
                  pl.BlockSpec((None, nh, tq), lambda i, a, c: (i, 0, a)),
                  pl.BlockSpec((None, tk, LANES), kmap)],
        out_specs=pl.BlockSpec((None, tq, d), lambda i, a, c: (i, a, 0)),
        out_shape=jax.ShapeDtypeStruct((b, t, d), BF16),
        scratch_shapes=[pltpu.VMEM((nh, tq), F32), pltpu.VMEM((nh, tq), F32), pltpu.VMEM((d, tq), F32)],
        compiler_params=_cparams("parallel", "parallel", "arbitrary"),
        name="fox_prompt_attention",
    )(q, k, vt, c_row, c_col)


def _fox_sample_kernel(pt_ref, q_ref, kn_ref, vn_ref, lfn_ref, ck_hbm, cv_hbm, *refs,
                       nh, hd, npp, scale, layer, npages):
    lf_refs = refs[:npp]
    o_ref, qbdt_s, crow_s, m_s, l_s, acc_s, carry_s, kbuf, vbuf, sem = refs[npp:]
    b = pl.program_id(0)
    j = pl.program_id(1)
    nb = pl.num_programs(0)
    nj = pl.num_programs(1)
    slot = lax.rem(j, 2)

    def page_copies(bb, jj, sl):
        cps = []
        for i in range(npp):
            phys = pt_ref[bb, npages - 1 - (jj * npp + i)]
            for h in range(nh):
                cps.append(pltpu.make_async_copy(ck_hbm.at[layer, phys, :, h, :], kbuf.at[sl, i, h], sem.at[sl, 0]))
                cps.append(pltpu.make_async_copy(cv_hbm.at[layer, phys, :, h, :], vbuf.at[sl, i, h], sem.at[sl, 1]))
        return cps

    @pl.when((b == 0) & (j == 0))
    def _():
        for cp in page_copies(b, j, slot):
            cp.start()

    is_last = (b == nb - 1) & (j == nj - 1)
    nxt_j = jnp.where(j == nj - 1, 0, j + 1)
    nxt_b = jnp.where(j == nj - 1, b + 1, b)

    @pl.when(jnp.logical_not(is_last))
    def _():
        for cp in page_copies(nxt_b, nxt_j, 1 - slot):
            cp.start()

    for cp in page_copies(b, j, slot):
        cp.wait()
    s_new = q_ref.shape[0]
    page = lf_refs[0].shape[1]
    nr = nh * s_new
    d = nh * hd
    npair = nh // 2

    expand_t = (lax.broadcasted_iota(jnp.int32, (nh, nr), 0)
                == _div_pow2(lax.broadcasted_iota(jnp.int32, (nh, nr), 1), s_new)).astype(F32)
    eye_r = lax.broadcasted_iota(jnp.int32, (nr, nr), 0) == lax.broadcasted_iota(jnp.int32, (nr, nr), 1)

    def update(st, v_pairs):
        m_prev = m_s[...]
        m_new = jnp.maximum(m_prev, jnp.max(st, axis=0, keepdims=True))
        alpha = jnp.exp(m_prev - m_new)
        pt = jnp.exp(st - m_new)
        l_s[...] = alpha * l_s[...] + jnp.sum(pt, axis=0, keepdims=True)
        m_s[...] = m_new
        p = pt.T.astype(BF16)
        alpha_col = jnp.sum(jnp.where(eye_r, alpha, 0.0), axis=1, keepdims=True)
        for c in range(npair):
            rs = slice(2 * s_new * c, 2 * s_new * (c + 1))
            acc_s[rs, :] = alpha_col[rs, :] * acc_s[rs, :] + jnp.dot(p[rs, :], v_pairs[c], preferred_element_type=F32)

    @pl.when(j == 0)
    def _():
        qrep = jnp.concatenate([q_ref[...].astype(F32)] * nh, axis=0)
        rh = _div_pow2(lax.broadcasted_iota(jnp.int32, (nr, d), 0), s_new)
        ch = _div_pow2(lax.broadcasted_iota(jnp.int32, (nr, d), 1), hd)
        qbdt_s[...] = jnp.where(rh == ch, qrep, 0.0).T.astype(BF16)
        carry_s[...] = jnp.zeros_like(carry_s)
        m_s[...] = jnp.full_like(m_s, NEG)
        l_s[...] = jnp.zeros_like(l_s)
        acc_s[...] = jnp.zeros_like(acc_s)

        si = lax.broadcasted_iota(jnp.int32, (LANES, LANES), 0)
        sj = lax.broadcasted_iota(jnp.int32, (LANES, LANES), 1)
        lower = (sj <= si).astype(F32)
        c_new_t = lax.dot_general(lower, lfn_ref[...], NT_DIMS, preferred_element_type=F32, precision=HIGHEST)
        c_b = jnp.dot(c_new_t, expand_t, preferred_element_type=F32, precision=HIGHEST)
        key = lax.broadcasted_iota(jnp.int32, (LANES, nr), 0)
        tok = _mod_pow2(lax.broadcasted_iota(jnp.int32, (LANES, nr), 1), s_new)
        crow = jnp.sum(jnp.where(key == tok, c_b, 0.0), axis=0, keepdims=True)
        crow_s[...] = crow
        pad = jnp.zeros((LANES - s_new, d), F32)
        kn = jnp.concatenate([kn_ref[...], pad], axis=0).astype(BF16)
        vn = jnp.concatenate([vn_ref[...], pad], axis=0).astype(BF16)
        st = jnp.dot(kn, qbdt_s[...], preferred_element_type=F32) * scale
        st = jnp.where((key < s_new) & (key <= tok), st + crow - c_b, NEG)
        update(st, [vn[:, 2 * hd * c:2 * hd * (c + 1)] for c in range(npair)])

    pi = lax.broadcasted_iota(jnp.int32, (page, page), 0)
    pj = lax.broadcasted_iota(jnp.int32, (page, page), 1)
    later = (pj > pi).astype(F32)
    ones = jnp.ones((8, page), F32)
    st_parts = []
    for i in range(npp):
        lfp = lf_refs[i][...]
        rev = lax.dot_general(later, lfp, NT_DIMS, preferred_element_type=F32, precision=HIGHEST) + carry_s[...]
        carry_s[...] = carry_s[...] + lax.dot_general(ones, lfp, NT_DIMS, preferred_element_type=F32,
                                                      precision=HIGHEST)[0:1, :]
        bias = jnp.dot(rev, expand_t, preferred_element_type=F32, precision=HIGHEST)
        kb = jnp.concatenate([kbuf[slot, i, h].astype(BF16) for h in range(nh)], axis=1)
        st = jnp.dot(kb, qbdt_s[...], preferred_element_type=F32) * scale
        st_parts.append(st + crow_s[...] + bias)
    v_pairs = []
    for c in range(npair):
        v_pairs.append(jnp.concatenate(
            [jnp.concatenate([vbuf[slot, i, 2 * c].astype(BF16),
                              vbuf[slot, i, 2 * c + 1].astype(BF16)], axis=1) for i in range(npp)], axis=0))
    update(jnp.concatenate(st_parts, axis=0), v_pairs)

    @pl.when(j == pl.num_programs(1) - 1)
    def _():
        l_col = jnp.sum(jnp.where(eye_r, l_s[...], 0.0), axis=1, keepdims=True)
        for h in range(nh):
            rs = slice(h * s_new, (h + 1) * s_new)
            cs = slice((h % 2) * hd, (h % 2 + 1) * hd)
            o_ref[:, h * hd:(h + 1) * hd] = (acc_s[rs, cs] / l_col[rs, :]).astype(o_ref.dtype)


def fox_sample_attention(q, k_new, v_new, lf_new, cache_k, cache_v, cache_lf, page_table, layer, nh, npp=4):
    bd, s_new, d = q.shape
    hd = d // nh
    page = cache_lf.shape[3]
    npages = page_table.shape[1]
    assert npages % npp == 0 and (npages // npp) % 2 == 0
    assert nh * s_new == LANES and page == LANES and nh % 2 == 0
    nr = nh * s_new

    def paged(rows, width, i):
        return pl.BlockSpec((None, None, rows, width),
                            lambda b, j, pt: (layer, pt[b, npages - 1 - (j * npp + i)], 0, 0))

    seq = lambda r, w: pl.BlockSpec((None, r, w), lambda b, j, pt: (b, 0, 0))
    grid_spec = pltpu.PrefetchScalarGridSpec(
        num_scalar_prefetch=1,
        grid=(bd, npages // npp),
        in_specs=[seq(s_new, d), seq(s_new, d), seq(s_new, d), seq(nh, LANES)]
                 + [pl.BlockSpec(memory_space=pl.ANY), pl.BlockSpec(memory_space=pl.ANY)]
                 + [paged(nh, page, i) for i in range(npp)],
        out_specs=seq(s_new, d),
        scratch_shapes=[pltpu.VMEM((d, nr), BF16), pltpu.VMEM((1, nr), F32), pltpu.VMEM((1, nr), F32),
                        pltpu.VMEM((1, nr), F32), pltpu.VMEM((nr, 2 * hd), F32), pltpu.VMEM((1, nh), F32),
                        pltpu.VMEM((2, npp, nh, page, hd), F32), pltpu.VMEM((2, npp, nh, page, hd), F32),
                        pltpu.SemaphoreType.DMA((2, 2))],
    )
    return pl.pallas_call(
        functools.partial(_fox_sample_kernel, nh=nh, hd=hd, npp=npp, scale=float(hd) ** -0.5,
                          layer=layer, npages=npages),
        grid_spec=grid_spec,
        out_shape=jax.ShapeDtypeStruct((bd, s_new, d), BF16),
        compiler_params=_cparams("arbitrary", "arbitrary"),
        name="fox_sample_attention",
    )(page_table, q, k_new, v_new, lf_new, cache_k, cache_v, *([cache_lf] * npp))


def kernel(x_prompt, x_sample, state_conv, state_C, state_n, state_m, cache_k, cache_v, cache_logf,
           page_table, meta_tokens, norm_mix_pre, norm_mix_post, norm_mlp_pre, norm_mlp_post,
           w_in_ab, conv_w, b_igate, b_fgate, mlstm_norm, w_out_ab, w_in_c, b_fox, w_out_c,
           w_mlp_up, w_mlp_down):
    bp, seq, d_model = x_prompt.shape
    bs, s_new, _ = x_sample.shape
    depth = norm_mix_pre.shape[0]
    n_meta = meta_tokens.shape[0]
    _, _, mh, hdb, _ = state_C.shape
    d_mlstm = mh * hdb
    d_conv = conv_w.shape[2]
    n_odd, n_pool, page, fh, fhd = cache_k.shape
    d_fox = fh * fhd
    tp = n_meta + seq

    meta = jnp.broadcast_to(meta_tokens[None], (bp, n_meta, d_model))
    xp = jnp.concatenate([meta, x_prompt], axis=1)
    xs = x_sample.reshape(1, bs * s_new, d_model)
    vec = lambda a: a.reshape(1, -1)

    clf = jnp.swapaxes(cache_logf, 2, 3)

    wt_ab = jnp.swapaxes(w_in_ab, 1, 2)
    wt_c = jnp.swapaxes(w_in_c, 1, 2)
    w_out_ab_b = w_out_ab.astype(BF16)
    w_out_c_b = w_out_c.astype(BF16)
    w_up_b = w_mlp_up.astype(BF16)
    w_down_b = w_mlp_down.astype(BF16)

    xpn = rmsnorm(xp, vec(norm_mix_pre[0]))
    xsn = rmsnorm(xs, vec(norm_mix_pre[0]))

    conv_p, c_p, n_p, m_p, conv_s, c_s, n_s, m_s = [], [], [], [], [], [], [], []
    k_p, v_p, lf_p, k_s, v_s, lf_s = [], [], [], [], [], []
    for layer in range(depth):
        if layer % 2 == 0:
            e = layer // 2
            n_main = 3 * d_conv + 4 * d_mlstm
            wg_t = wt_ab[e, n_main:, :]
            b_g = jnp.concatenate([b_igate[e], b_fgate[e]]).reshape(2 * mh, 1)
            nh_vec = vec(mlstm_norm[e])

            def mix(xn, prev, c0, n0, m0, chunk, nb):
                t = xn.shape[1]
                proj = matmul_cols(xn, wt_ab, e, 0, n_main, BF16)
                g = gates(xn, wg_t, b_g, mh)
                proj = proj.reshape(nb, t // (nb // xn.shape[0]), n_main)
                tl = proj.shape[1]
                g = g.reshape(xn.shape[0], 2, mh, nb // xn.shape[0], tl)
                gh = jnp.transpose(g, (0, 3, 2, 1, 4)).reshape(nb, mh, 2, tl)
                if tl < chunk:
                    gh = jnp.pad(gh, ((0, 0), (0, 0), (0, 0), (0, chunk - tl)))
                ya, cnew = conv_mixer(proj, prev, conv_w[e])
                yb, c1, n1, m1 = mlstm_mixer(proj, gh, c0, n0.reshape(nb, mh, 1, hdb), m0.reshape(nb, mh, 1, 1),
                                             nh_vec, 3 * d_conv, chunk)
                shp = (xn.shape[0], t, -1)
                return ya.reshape(shp), yb.reshape(shp), cnew, c1, n1.reshape(nb, mh, hdb), m1.reshape(nb, mh)

            zc = jnp.zeros((bp, mh, hdb, hdb), F32)
            ya, yb, cvp, c1, n1, m1 = mix(xpn, jnp.zeros((bp, 2, d_conv), F32), zc,
                                          jnp.zeros((bp, mh, hdb), F32), jnp.full((bp, mh), M_INIT, F32), 256, bp)
            conv_p.append(cvp); c_p.append(c1); n_p.append(n1); m_p.append(m1)
            xp = outproj_residual([ya, yb], w_out_ab_b, e, vec(norm_mix_post[layer]), xp)
            ya, yb, cvs, c1, n1, m1 = mix(xsn, state_conv[e], state_C[e], state_n[e], state_m[e], LANES, bs)
            conv_s.append(cvs); c_s.append(c1); n_s.append(n1); m_s.append(m1)
            xs = outproj_residual([ya, yb], w_out_ab_b, e, vec(norm_mix_post[layer]), xs)
        else:
            o = layer // 2
            wg_t = wt_c[o, 3 * d_fox:, :]
            b_g = b_fox[o].reshape(fh, 1)

            q = matmul_cols(xpn, wt_c, o, 0, d_fox, BF16)
            k = matmul_cols(xpn, wt_c, o, d_fox, d_fox, F32)
            v, vt = matmul_cols(xpn, wt_c, o, 2 * d_fox, d_fox, F32, tile=1408, transposed_copy=True)
            lf = gates(xpn, wg_t, b_g, 0)
            c_row, c_col = fox_cumsum(lf)
            att = fox_prompt_attention(q, k, vt, c_row, c_col, fh)
            k_p.append(k.reshape(bp, tp, fh, fhd)); v_p.append(v.reshape(bp, tp, fh, fhd))
            lf_p.append(jnp.transpose(lf, (0, 2, 1)))
            xp = outproj_residual([att], w_out_c_b, o, vec(norm_mix_post[layer]), xp)

            q = matmul_cols(xsn, wt_c, o, 0, d_fox, BF16).reshape(bs, s_new, d_fox)
            k = matmul_cols(xsn, wt_c, o, d_fox, d_fox, F32).reshape(bs, s_new, d_fox)
            v = matmul_cols(xsn, wt_c, o, 2 * d_fox, d_fox, F32).reshape(bs, s_new, d_fox)
            lf = gates(xsn, wg_t, b_g, 0)
            lf = jnp.transpose(lf.reshape(fh, bs, s_new), (1, 0, 2))
            lf_pad = jnp.pad(lf, ((0, 0), (0, 0), (0, LANES - s_new)))
            att = fox_sample_attention(q, k, v, lf_pad, cache_k, cache_v, clf, page_table, o, fh)
            k_s.append(k.reshape(bs, s_new, fh, fhd)); v_s.append(v.reshape(bs, s_new, fh, fhd))
            lf_s.append(jnp.transpose(lf, (0, 2, 1)))
            xs = outproj_residual([att.reshape(1, bs * s_new, d_fox)], w_out_c_b, o, vec(norm_mix_post[layer]), xs)

        g_next = vec(norm_mix_pre[layer + 1]) if layer + 1 < depth else vec(norm_mix_pre[layer])
        xp, xpn = mlp_residual(xp, vec(norm_mlp_pre[layer]), w_up_b, w_down_b, layer, vec(norm_mlp_post[layer]), g_next)
        xs, xsn = mlp_residual(xs, vec(norm_mlp_pre[layer]), w_up_b, w_down_b, layer, vec(norm_mlp_post[layer]), g_next)

    y_prompt = xp[:, n_meta:]
    y_sample = xs.reshape(bs, s_new, d_model)
    return (y_prompt, y_sample,
            jnp.stack(conv_p), jnp.stack(c_p), jnp.stack(n_p), jnp.stack(m_p),
            jnp.stack(k_p), jnp.stack(v_p), jnp.stack(lf_p),
            jnp.stack(conv_s), jnp.stack(c_s), jnp.stack(n_s), jnp.stack(m_s),
            jnp.stack(k_s), jnp.stack(v_s), jnp.stack(lf_s))
```

```python
import functools

import jax
import jax.numpy as jnp
from jax import lax
from jax.experimental import pallas as pl
from jax.experimental.pallas import tpu as pltpu

F32 = jnp.float32
BF16 = jnp.bfloat16
EPS = 1e-6
M_INIT = -1e30
NEG = -1e30
HIGHEST = lax.Precision.HIGHEST
LOG2E = 1.4426950408889634

V7X_VMEM_BYTES = 64 * 1024 * 1024
VMEM_LIMIT = V7X_VMEM_BYTES - 8 * 1024 * 1024
LANES = 128
BF16_SUBLANES = 16

NT_DIMS = (((1,), (1,)), ((), ()))


def _cparams(*sem):
    return pltpu.CompilerParams(dimension_semantics=sem, vmem_limit_bytes=VMEM_LIMIT)


def _round_up(x, m):
    return -(-x // m) * m


def _row_tile(t, target, align=BF16_SUBLANES):
    if t <= target:
        return t
    n = pl.cdiv(t, target)
    return _round_up(pl.cdiv(t, n), align)


def _rms(x, g):
    ms = jnp.mean(x * x, axis=-1, keepdims=True)
    return x * lax.rsqrt(ms + EPS) * g


def _div_pow2(x, n):
    assert n & (n - 1) == 0
    return lax.shift_right_logical(x, n.bit_length() - 1)


def _mod_pow2(x, n):
    assert n & (n - 1) == 0
    return lax.bitwise_and(x, n - 1)


def _log_sigmoid(x):
    return jnp.minimum(x, 0.0) - jnp.log1p(jnp.exp(-jnp.abs(x)))


def _rmsnorm_kernel(x_ref, g_ref, o_ref):
    o_ref[...] = _rms(x_ref[...], g_ref[...]).astype(o_ref.dtype)


def rmsnorm(x, g, tile=1024):
    b, t, d = x.shape
    tm = _row_tile(t, tile)
    return pl.pallas_call(
        _rmsnorm_kernel,
        grid=(b, pl.cdiv(t, tm)),
        in_specs=[pl.BlockSpec((None, tm, d), lambda i, j: (i, j, 0)),
                  pl.BlockSpec((1, d), lambda i, j: (0, 0))],
        out_specs=pl.BlockSpec((None, tm, d), lambda i, j: (i, j, 0)),
        out_shape=jax.ShapeDtypeStruct((b, t, d), BF16),
        compiler_params=_cparams("parallel", "parallel"),
        name="rmsnorm",
    )(x, g)


def _matmul_kernel(x_ref, wt_ref, o_ref, *rest, bf16_copy, transposed_copy):
    wb_ref = rest[-1]

    @pl.when((pl.program_id(1) == 0) & (pl.program_id(2) == 0))
    def _():
        wb_ref[...] = wt_ref[...].astype(BF16)

    y = lax.dot_general(x_ref[...], wb_ref[...], NT_DIMS, preferred_element_type=F32)
    o_ref[...] = y.astype(o_ref.dtype)
    if bf16_copy:
        rest[0][...] = y.astype(BF16)
    if transposed_copy:
        rest[0][...] = y.T.astype(BF16)


def matmul_cols(x, wt, layer, col0, ncols, out_dtype, tn=1024, tile=1376, bf16_copy=False, transposed_copy=False):
    b, t, k = x.shape
    assert not (bf16_copy and transposed_copy)
    tm = _row_tile(t, tile, LANES if transposed_copy else BF16_SUBLANES)
    assert col0 % tn == 0 and ncols % tn == 0
    c0 = col0 // tn
    out_specs = [pl.BlockSpec((None, tm, tn), lambda n, i, j: (i, j, n))]
    out_shape = [jax.ShapeDtypeStruct((b, t, ncols), out_dtype)]
    if bf16_copy:
        out_specs.append(out_specs[0])
        out_shape.append(jax.ShapeDtypeStruct((b, t, ncols), BF16))
    if transposed_copy:
        assert tm % LANES == 0
        out_specs.append(pl.BlockSpec((None, tn, tm), lambda n, i, j: (i, n, j)))
        out_shape.append(jax.ShapeDtypeStruct((b, ncols, t), BF16))
    res = pl.pallas_call(
        functools.partial(_matmul_kernel, bf16_copy=bf16_copy, transposed_copy=transposed_copy),
        grid=(ncols // tn, b, pl.cdiv(t, tm)),
        in_specs=[pl.BlockSpec((None, tm, k), lambda n, i, j: (i, j, 0)),
                  pl.BlockSpec((None, tn, k), lambda n, i, j: (layer, c0 + n, 0))],
        out_specs=out_specs,
        out_shape=out_shape,
        scratch_shapes=[pltpu.VMEM((tn, k), BF16)],
        compiler_params=_cparams("arbitrary", "arbitrary", "arbitrary"),
        name="matmul_cols",
    )(x, wt)
    return res if (bf16_copy or transposed_copy) else res[0]


def _gates_kernel(x_ref, wt_ref, b_ref, o_ref, *, n_plain):
    g = lax.dot_general(wt_ref[...].astype(BF16), x_ref[...], NT_DIMS, preferred_element_type=F32)
    g = g + b_ref[...]
    row = lax.broadcasted_iota(jnp.int32, g.shape, 0)
    o_ref[...] = jnp.where(row < n_plain, g, _log_sigmoid(g))


def gates(x, wt, bias, n_plain, tile=1024):
    b, t, k = x.shape
    ng = wt.shape[0]
    tm = t if t <= tile else tile
    return pl.pallas_call(
        functools.partial(_gates_kernel, n_plain=n_plain),
        grid=(b, pl.cdiv(t, tm)),
        in_specs=[pl.BlockSpec((None, tm, k), lambda i, j: (i, j, 0)),
                  pl.BlockSpec((ng, k), lambda i, j: (0, 0)),
                  pl.BlockSpec((ng, 1), lambda i, j: (0, 0))],
        out_specs=pl.BlockSpec((None, ng, tm), lambda i, j: (i, 0, j)),
        out_shape=jax.ShapeDtypeStruct((b, ng, t), F32),
        compiler_params=_cparams("parallel", "parallel"),
        name="gates",
    )(x, wt, bias)


def _conv_kernel(ab_ref, ac_ref, ax_ref, prev_ref, w_ref, y_ref, new_ref, carry_ref, *, tail):
    j = pl.program_id(1)

    @pl.when(j == 0)
    def _():
        carry_ref[...] = prev_ref[...]

    u = ac_ref[...].astype(F32) * ax_ref[...].astype(F32)
    tc = u.shape[0]
    row = lax.broadcasted_iota(jnp.int32, u.shape, 0)
    c0 = carry_ref[0:1, :]
    c1 = carry_ref[1:2, :]
    u1 = jnp.where(row == 0, c1, pltpu.roll(u, 1, axis=0))
    u2 = jnp.where(row == 0, c0, jnp.where(row == 1, c1, pltpu.roll(u, 2, axis=0)))
    z = w_ref[0:1, :] * u2 + w_ref[1:2, :] * u1 + w_ref[2:3, :] * u
    y_ref[...] = (ab_ref[...].astype(F32) * z).astype(y_ref.dtype)
    carry_ref[...] = u[tc - 2:tc, :]

    @pl.when(j == pl.num_programs(1) - 1)
    def _():
        new_ref[...] = u[tail:tail + 2, :]


def conv_mixer(p, prev, w, tile=688):
    b, t, _ = p.shape
    c = w.shape[1]
    tc = _row_tile(t, tile)
    nt = pl.cdiv(t, tc)
    tail = (t - 2) - (nt - 1) * tc
    assert 0 <= tail and tail + 2 <= tc
    return pl.pallas_call(
        functools.partial(_conv_kernel, tail=tail),
        grid=(b, nt),
        in_specs=[pl.BlockSpec((None, tc, c), lambda i, j: (i, j, 0)),
                  pl.BlockSpec((None, tc, c), lambda i, j: (i, j, 1)),
                  pl.BlockSpec((None, tc, c), lambda i, j: (i, j, 2)),
                  pl.BlockSpec((None, 2, c), lambda i, j: (i, 0, 0)),
                  pl.BlockSpec((3, c), lambda i, j: (0, 0))],
        out_specs=[pl.BlockSpec((None, tc, c), lambda i, j: (i, j, 0)),
                   pl.BlockSpec((None, 2, c), lambda i, j: (i, 0, 0))],
        out_shape=[jax.ShapeDtypeStruct((b, t, c), BF16),
                   jax.ShapeDtypeStruct((b, 2, c), F32)],
        scratch_shapes=[pltpu.VMEM((2, c), F32)],
        compiler_params=_cparams("parallel", "arbitrary"),
        name="conv_mixer",
    )(p, p, p, prev, w)


def _mlstm_kernel(q_ref, k_ref, v_ref, o_ref, g_ref, c0_ref, n0_ref, m0_ref, nh_ref,
                  y_ref, cout_ref, nout_ref, mout_ref, c_s, n_s, m_s, *, chunk, total, scale):
    ci = pl.program_id(1)

    @pl.when(ci == 0)
    def _():
        c_s[...] = c0_ref[...]
        n_s[...] = n0_ref[...]
        m_s[...] = m0_ref[...]

    lb = q_ref.shape[0]
    nheads, dv, dk = c_s.shape
    valid = total - ci * chunk
    rows = lax.broadcasted_iota(jnp.int32, (chunk, 1), 0)
    lane = lax.broadcasted_iota(jnp.int32, (1, chunk), 1)
    ri = lax.broadcasted_iota(jnp.int32, (chunk, chunk), 0)
    cj = lax.broadcasted_iota(jnp.int32, (chunk, chunk), 1)
    causal = cj <= ri
    eye = cj == ri

    def load(ref, sl):
        x = ref[:, sl].astype(F32)
        if lb < chunk:
            x = jnp.concatenate([x, jnp.zeros((chunk - lb, x.shape[1]), F32)], axis=0)
        return jnp.where(rows < valid, x, 0.0)

    for h in range(nheads):
        sk = slice(h * dk, (h + 1) * dk)
        sv = slice(h * dv, (h + 1) * dv)
        q = load(q_ref, sk)
        k = load(k_ref, sk) * scale
        v = load(v_ref, sv)
        qb = q.astype(BF16)
        kb = k.astype(BF16)

        li = jnp.where(lane < valid, g_ref[h, 0:1, :], NEG)
        lf = jnp.where(lane < valid, g_ref[h, 1:2, :], 0.0)
        b_col = jnp.sum(jnp.where(causal, lf, 0.0), axis=1, keepdims=True)
        b_row = jnp.sum(jnp.where(eye, b_col, 0.0), axis=0, keepdims=True)
        li_col = jnp.sum(jnp.where(eye, li, 0.0), axis=1, keepdims=True)

        m_prev = m_s[h]
        c_prev = c_s[h]
        n_prev = n_s[h]
        d = jnp.where(causal, b_col - b_row + li, NEG)
        g_col = m_prev + b_col
        m_t = jnp.maximum(g_col, jnp.max(d, axis=1, keepdims=True))
        s = lax.dot_general(qb, kb, NT_DIMS, preferred_element_type=F32) * jnp.exp(d - m_t)
        inter = jnp.exp(g_col - m_t)
        cq = lax.dot_general(qb, c_prev.astype(BF16), NT_DIMS, preferred_element_type=F32)
        num = jnp.dot(s.astype(BF16), v.astype(BF16), preferred_element_type=F32) + inter * cq
        den = jnp.sum(s, axis=1, keepdims=True) + inter * jnp.sum(q * n_prev, axis=1, keepdims=True)
        hh = num / jnp.maximum(jnp.abs(den), jnp.exp(-m_t))

        m_new = m_t[chunk - 1:chunk, :]
        b_last = b_row[:, chunk - 1:chunk]
        w_col = jnp.exp(b_last - b_col + li_col - m_new)
        decay = jnp.exp(m_prev + b_last - m_new)
        vw = (v * w_col).T.astype(BF16)
        c_s[h] = decay * c_prev + jnp.dot(vw, kb, preferred_element_type=F32)
        n_s[h] = decay * n_prev + jnp.sum(w_col * k, axis=0, keepdims=True)
        m_s[h] = m_new

        mu = jnp.mean(hh, axis=1, keepdims=True)
        hc = hh - mu
        var = jnp.mean(hc * hc, axis=1, keepdims=True)
        hn = hc * lax.rsqrt(var + EPS) * nh_ref[:, sv]
        y = jax.nn.sigmoid(o_ref[:, sv].astype(F32)) * hn[:lb]
        y_ref[:, sv] = y.astype(y_ref.dtype)

    @pl.when(ci == pl.num_programs(1) - 1)
    def _():
        cout_ref[...] = c_s[...]
        nout_ref[...] = n_s[...]
        mout_ref[...] = m_s[...]


def mlstm_mixer(p, gh, c0, n0, m0, norm_h, col0, chunk):
    b, t, _ = p.shape
    _, h, dv, dk = c0.shape
    assert dv == dk and col0 % (h * dk) == 0
    lb = min(chunk, t)
    nc = pl.cdiv(t, chunk)
    blk0 = col0 // (h * dk)

    def pspec(group):
        return pl.BlockSpec((None, lb, h * dk), lambda i, c: (i, c, blk0 + group))

    state4 = lambda r, cdim: pl.BlockSpec((None, h, r, cdim), lambda i, c: (i, 0, 0, 0))
    return pl.pallas_call(
        functools.partial(_mlstm_kernel, chunk=chunk, total=t, scale=float(dk) ** -0.5),
        grid=(b, nc),
        in_specs=[pspec(0), pspec(1), pspec(2), pspec(3),
                  pl.BlockSpec((None, h, 2, chunk), lambda i, c: (i, 0, 0, c)),
                  state4(dv, dk), state4(1, dk), state4(1, 1),
                  pl.BlockSpec((1, h * dv), lambda i, c: (0, 0))],
        out_specs=[pl.BlockSpec((None, lb, h * dv), lambda i, c: (i, c, 0)),
                   state4(dv, dk), state4(1, dk), state4(1, 1)],
        out_shape=[jax.ShapeDtypeStruct((b, t, h * dv), BF16),
                   jax.ShapeDtypeStruct((b, h, dv, dk), F32),
                   jax.ShapeDtypeStruct((b, h, 1, dk), F32),
                   jax.ShapeDtypeStruct((b, h, 1, 1), F32)],
        scratch_shapes=[pltpu.VMEM((h, dv, dk), F32), pltpu.VMEM((h, 1, dk), F32), pltpu.VMEM((h, 1, 1), F32)],
        compiler_params=_cparams("parallel", "arbitrary"),
        name="mlstm_mixer",
    )(p, p, p, p, gh, c0, n0, m0, norm_h)


def _outproj_kernel(*refs, n_in):
    a_refs = refs[:n_in]
    w_ref, g_ref, x_ref, o_ref = refs[n_in:]
    y = None
    k0 = 0
    for a_ref in a_refs:
        kk = a_ref.shape[1]
        part = jnp.dot(a_ref[...], w_ref[k0:k0 + kk, :], preferred_element_type=F32)
        y = part if y is None else y + part
        k0 += kk
    o_ref[...] = x_ref[...] + _rms(y, g_ref[...])


def outproj_residual(a_list, w, layer, g, x, tile=688):
    b, t, d = x.shape
    tm = _row_tile(t, tile)
    kt = w.shape[1]
    a_specs = [pl.BlockSpec((None, tm, a.shape[2]), lambda i, j: (i, j, 0)) for a in a_list]
    return pl.pallas_call(
        functools.partial(_outproj_kernel, n_in=len(a_list)),
        grid=(b, pl.cdiv(t, tm)),
        in_specs=a_specs + [pl.BlockSpec((None, kt, d), lambda i, j: (layer, 0, 0)),
                            pl.BlockSpec((1, d), lambda i, j: (0, 0)),
                            pl.BlockSpec((None, tm, d), lambda i, j: (i, j, 0))],
        out_specs=pl.BlockSpec((None, tm, d), lambda i, j: (i, j, 0)),
        out_shape=jax.ShapeDtypeStruct((b, t, d), F32),
        compiler_params=_cparams("parallel", "parallel"),
        name="outproj_residual",
    )(*a_list, w, g, x)


def _mlp_kernel(x_ref, g1_ref, wu_ref, wd_ref, g2_ref, g3_ref, o_ref, on_ref, xn_s, acc_s):
    f = pl.program_id(2)

    @pl.when(f == 0)
    def _():
        xn_s[...] = _rms(x_ref[...], g1_ref[...]).astype(BF16)
        acc_s[...] = jnp.zeros_like(acc_s)

    h = jnp.maximum(jnp.dot(xn_s[...], wu_ref[...], preferred_element_type=F32), 0.0)
    acc_s[...] += jnp.dot((h * h).astype(BF16), wd_ref[...], preferred_element_type=F32)

    @pl.when(f == pl.num_programs(2) - 1)
    def _():
        xnew = x_ref[...] + _rms(acc_s[...], g2_ref[...])
        o_ref[...] = xnew
        on_ref[...] = _rms(xnew, g3_ref[...]).astype(on_ref.dtype)


def mlp_residual(x, g_pre, w_up, w_down, layer, g_post, g_next, tile=688, tf=512):
    b, t, d = x.shape
    f = w_up.shape[2]
    tm = _row_tile(t, tile)
    row = pl.BlockSpec((None, tm, d), lambda i, j, l: (i, j, 0))
    vec = pl.BlockSpec((1, d), lambda i, j, l: (0, 0))
    return pl.pallas_call(
        _mlp_kernel,
        grid=(b, pl.cdiv(t, tm), f // tf),
        in_specs=[row, vec,
                  pl.BlockSpec((None, d, tf), lambda i, j, l: (layer, 0, l)),
                  pl.BlockSpec((None, tf, d), lambda i, j, l: (layer, l, 0)),
                  vec, vec],
        out_specs=[row, row],
        out_shape=[jax.ShapeDtypeStruct((b, t, d), F32), jax.ShapeDtypeStruct((b, t, d), BF16)],
        scratch_shapes=[pltpu.VMEM((tm, d), BF16), pltpu.VMEM((tm, d), F32)],
        compiler_params=_cparams("parallel", "parallel", "arbitrary"),
        name="mlp_residual",
    )(x, g_pre, w_up, w_down, g_post, g_next)


def _fox_cumsum_kernel(lf_ref, crow_ref, ccol_ref, carry_ref):
    j = pl.program_id(1)

    @pl.when(j == 0)
    def _():
        carry_ref[...] = jnp.zeros_like(carry_ref)

    nh, tc = lf_ref.shape
    ri = lax.broadcasted_iota(jnp.int32, (tc, tc), 0)
    cj = lax.broadcasted_iota(jnp.int32, (tc, tc), 1)
    causal = cj <= ri
    eye = cj == ri
    lane = lax.broadcasted_iota(jnp.int32, (tc, LANES), 1)
    cols = jnp.zeros((tc, LANES), F32)
    for h in range(nh):
        col = jnp.sum(jnp.where(causal, lf_ref[h:h + 1, :], 0.0), axis=1, keepdims=True)
        col = col + carry_ref[h:h + 1, :]
        crow_ref[h:h + 1, :] = jnp.sum(jnp.where(eye, col, 0.0), axis=0, keepdims=True) * LOG2E
        carry_ref[h:h + 1, :] = col[tc - 1:tc, :]
        cols = jnp.where(lane == h, col, cols)
    ccol_ref[...] = cols * LOG2E


def fox_cumsum(lf, tile=512):
    b, nh, t = lf.shape
    tc = tile
    return pl.pallas_call(
        _fox_cumsum_kernel,
        grid=(b, pl.cdiv(t, tc)),
        in_specs=[pl.BlockSpec((None, nh, tc), lambda i, j: (i, 0, j))],
        out_specs=[pl.BlockSpec((None, nh, tc), lambda i, j: (i, 0, j)),
                   pl.BlockSpec((None, tc, LANES), lambda i, j: (i, j, 0))],
        out_shape=[jax.ShapeDtypeStruct((b, nh, t), F32), jax.ShapeDtypeStruct((b, t, LANES), F32)],
        scratch_shapes=[pltpu.VMEM((nh, 1), F32)],
        compiler_params=_cparams("parallel", "arbitrary"),
        name="fox_cumsum",
    )(lf)


def _fox_prompt_kernel(q_ref, k_ref, vt_ref, cq_ref, ck_ref, o_ref, m_s, l_s, acc_s, *, nh, hd, total, scale):
    qi = pl.program_id(1)
    ki = pl.program_id(2)
    tq = q_ref.shape[0]
    tk = k_ref.shape[0]

    @pl.when(ki == 0)
    def _():
        m_s[...] = jnp.full_like(m_s, NEG)
        l_s[...] = jnp.zeros_like(l_s)
        acc_s[...] = jnp.zeros_like(acc_s)

    def step(diagonal):
        if diagonal:
            kpos = ki * tk + lax.broadcasted_iota(jnp.int32, (tk, tq), 0)
            qpos = qi * tq + lax.broadcasted_iota(jnp.int32, (tk, tq), 1)
            visible = kpos <= qpos
            vcol_ok = (ki * tk + lax.broadcasted_iota(jnp.int32, (1, tk), 1)) < total
        for h in range(nh):
            sl = slice(h * hd, (h + 1) * hd)
            st = lax.dot_general(k_ref[:, sl], q_ref[:, sl], NT_DIMS, preferred_element_type=F32)
            st = st * (scale * LOG2E) + cq_ref[h:h + 1, :] - ck_ref[:, h:h + 1]
            vt = vt_ref[sl, :]
            if diagonal:
                st = jnp.where(visible, st, NEG)
                vt = jnp.where(vcol_ok, vt, jnp.zeros_like(vt))
            m_prev = m_s[h:h + 1, :]
            m_new = jnp.maximum(m_prev, jnp.max(st, axis=0, keepdims=True))
            alpha = jnp.exp2(m_prev - m_new)
            p = jnp.exp2(st - m_new)
            l_s[h:h + 1, :] = alpha * l_s[h:h + 1, :] + jnp.sum(p, axis=0, keepdims=True)
            acc_s[sl, :] = alpha * acc_s[sl, :] + jnp.dot(vt, p.astype(BF16), preferred_element_type=F32)
            m_s[h:h + 1, :] = m_new

    @pl.when(ki < qi)
    def _():
        step(False)

    @pl.when(ki == qi)
    def _():
        step(True)
        for h in range(nh):
            sl = slice(h * hd, (h + 1) * hd)
            o_ref[:, sl] = (acc_s[sl, :] / l_s[h:h + 1, :]).T.astype(o_ref.dtype)


def fox_prompt_attention(q, k, vt, c_row, c_col, nh, tile=512):
    b, t, d = q.shape
    hd = d // nh
    tq = tk = tile
    nq = pl.cdiv(t, tq)
    kmap = lambda i, a, c: (i, jnp.minimum(c, a), 0)
    return pl.pallas_call(
        functools.partial(_fox_prompt_kernel, nh=nh, hd=hd, total=t, scale=float(hd) ** -0.5),
        grid=(b, nq, nq),
        in_specs=[pl.BlockSpec((None, tq, d), lambda i, a, c: (i, a, 0)),
                  pl.BlockSpec((None, tk, d), kmap),
                  pl.BlockSpec((None, d, tk), lambda i, a, c: (i, 0, jnp.minimum(c, a))),
                  pl.BlockSpec((None, nh, tq), lambda i, a, c: (i, 0, a)),
                  pl.BlockSpec((None, tk, LANES), kmap)],
        out_specs=pl.BlockSpec((None, tq, d), lambda i, a, c: (i, a, 0)),
        out_shape=jax.ShapeDtypeStruct((b, t, d), BF16),
        scratch_shapes=[pltpu.VMEM((nh, tq), F32), pltpu.VMEM((nh, tq), F32), pltpu.VMEM((d, tq), F32)],
        compiler_params=_cparams("parallel", "parallel", "arbitrary"),
        name="fox_prompt_attention",
    )(q, k, vt, c_row, c_col)


def _fox_sample_kernel(pt_ref, q_ref, kn_ref, vn_ref, lfn_ref, ck_hbm, cv_hbm, *refs,
                       nh, hd, npp, scale, layer, npages):
    lf_refs = refs[:npp]
    o_ref, qbdt_s, crow_s, m_s, l_s, acc_s, carry_s, kbuf, vbuf, sem = refs[npp:]
    b = pl.program_id(0)
    j = pl.program_id(1)
    nb = pl.num_programs(0)
    nj = pl.num_programs(1)
    slot = lax.rem(j, 2)

    def page_copies(bb, jj, sl):
        cps = []
        for i in range(npp):
            phys = pt_ref[bb, npages - 1 - (jj * npp + i)]
            for h in range(nh):
                cps.append(pltpu.make_async_copy(ck_hbm.at[layer, phys, :, h, :], kbuf.at[sl, i, h], sem.at[sl, 0]))
                cps.append(pltpu.make_async_copy(cv_hbm.at[layer, phys, :, h, :], vbuf.at[sl, i, h], sem.at[sl, 1]))
        return cps

    @pl.when((b == 0) & (j == 0))
    def _():
        for cp in page_copies(b, j, slot):
            cp.start()

    is_last = (b == nb - 1) & (j == nj - 1)
    nxt_j = jnp.where(j == nj - 1, 0, j + 1)
    nxt_b = jnp.where(j == nj - 1, b + 1, b)

    @pl.when(jnp.logical_not(is_last))
    def _():
        for cp in page_copies(nxt_b, nxt_j, 1 - slot):
            cp.start()

    for cp in page_copies(b, j, slot):
        cp.wait()
    s_new = q_ref.shape[0]
    page = lf_refs[0].shape[1]
    nr = nh * s_new
    d = nh * hd
    npair = nh // 2

    expand_t = (lax.broadcasted_iota(jnp.int32, (nh, nr), 0)
                == _div_pow2(lax.broadcasted_iota(jnp.int32, (nh, nr), 1), s_new)).astype(F32)
    eye_r = lax.broadcasted_iota(jnp.int32, (nr, nr), 0) == lax.broadcasted_iota(jnp.int32, (nr, nr), 1)

    def update(st, v_pairs):
        m_prev = m_s[...]
        m_new = jnp.maximum(m_prev, jnp.max(st, axis=0, keepdims=True))
        alpha = jnp.exp(m_prev - m_new)
        pt = jnp.exp(st - m_new)
        l_s[...] = alpha * l_s[...] + jnp.sum(pt, axis=0, keepdims=True)
        m_s[...] = m_new
        p = pt.T.astype(BF16)
        alpha_col = jnp.sum(jnp.where(eye_r, alpha, 0.0), axis=1, keepdims=True)
        for c in range(npair):
            rs = slice(2 * s_new * c, 2 * s_new * (c + 1))
            acc_s[rs, :] = alpha_col[rs, :] * acc_s[rs, :] + jnp.dot(p[rs, :], v_pairs[c], preferred_element_type=F32)

    @pl.when(j == 0)
    def _():
        qrep = jnp.concatenate([q_ref[...].astype(F32)] * nh, axis=0)
        rh = _div_pow2(lax.broadcasted_iota(jnp.int32, (nr, d), 0), s_new)
        ch = _div_pow2(lax.broadcasted_iota(jnp.int32, (nr, d), 1), hd)
        qbdt_s[...] = jnp.where(rh == ch, qrep, 0.0).T.astype(BF16)
        carry_s[...] = jnp.zeros_like(carry_s)
        m_s[...] = jnp.full_like(m_s, NEG)
        l_s[...] = jnp.zeros_like(l_s)
        acc_s[...] = jnp.zeros_like(acc_s)

        si = lax.broadcasted_iota(jnp.int32, (LANES, LANES), 0)
        sj = lax.broadcasted_iota(jnp.int32, (LANES, LANES), 1)
        lower = (sj <= si).astype(F32)
        c_new_t = lax.dot_general(lower, lfn_ref[...], NT_DIMS, preferred_element_type=F32, precision=HIGHEST)
        c_b = jnp.dot(c_new_t, expand_t, preferred_element_type=F32, precision=HIGHEST)
        key = lax.broadcasted_iota(jnp.int32, (LANES, nr), 0)
        tok = _mod_pow2(lax.broadcasted_iota(jnp.int32, (LANES, nr), 1), s_new)
        crow = jnp.sum(jnp.where(key == tok, c_b, 0.0), axis=0, keepdims=True)
        crow_s[...] = crow
        pad = jnp.zeros((LANES - s_new, d), F32)
        kn = jnp.concatenate([kn_ref[...], pad], axis=0).astype(BF16)
        vn = jnp.concatenate([vn_ref[...], pad], axis=0).astype(BF16)
        st = jnp.dot(kn, qbdt_s[...], preferred_element_type=F32) * scale
        st = jnp.where((key < s_new) & (key <= tok), st + crow - c_b, NEG)
        update(st, [vn[:, 2 * hd * c:2 * hd * (c + 1)] for c in range(npair)])

    pi = lax.broadcasted_iota(jnp.int32, (page, page), 0)
    pj = lax.broadcasted_iota(jnp.int32, (page, page), 1)
    later = (pj > pi).astype(F32)
    ones = jnp.ones((8, page), F32)
    st_parts = []
    for i in range(npp):
        lfp = lf_refs[i][...]
        rev = lax.dot_general(later, lfp, NT_DIMS, preferred_element_type=F32, precision=HIGHEST) + carry_s[...]
        carry_s[...] = carry_s[...] + lax.dot_general(ones, lfp, NT_DIMS, preferred_element_type=F32,
                                                      precision=HIGHEST)[0:1, :]
        bias = jnp.dot(rev, expand_t, preferred_element_type=F32, precision=HIGHEST)
        kb = jnp.concatenate([kbuf[slot, i, h].astype(BF16) for h in range(nh)], axis=1)
        st = jnp.dot(kb, qbdt_s[...], preferred_element_type=F32) * scale
        st_parts.append(st + crow_s[...] + bias)
    v_pairs = []
    for c in range(npair):
        v_pairs.append(jnp.concatenate(
            [jnp.concatenate([vbuf[slot, i, 2 * c].astype(BF16),
                              vbuf[slot, i, 2 * c + 1].astype(BF16)], axis=1) for i in range(npp)], axis=0))
    update(jnp.concatenate(st_parts, axis=0), v_pairs)

    @pl.when(j == pl.num_programs(1) - 1)
    def _():
        l_col = jnp.sum(jnp.where(eye_r, l_s[...], 0.0), axis=1, keepdims=True)
        for h in range(nh):
            rs = slice(h * s_new, (h + 1) * s_new)
            cs = slice((h % 2) * hd, (h % 2 + 1) * hd)
            o_ref[:, h * hd:(h + 1) * hd] = (acc_s[rs, cs] / l_col[rs, :]).astype(o_ref.dtype)


def fox_sample_attention(q, k_new, v_new, lf_new, cache_k, cache_v, cache_lf, page_table, layer, nh, max_npp=8):
    bd, s_new, d = q.shape
    hd = d // nh
    page = cache_lf.shape[3]
    npages = page_table.shape[1]
    npp = max(n for n in range(1, max_npp + 1) if npages % n == 0 and (npages // n) % 2 == 0)
    assert nh * s_new == LANES and page == LANES and nh % 2 == 0
    nr = nh * s_new

    def paged(rows, width, i):
        return pl.BlockSpec((None, None, rows, width),
                            lambda b, j, pt: (layer, pt[b, npages - 1 - (j * npp + i)], 0, 0))

    seq = lambda r, w: pl.BlockSpec((None, r, w), lambda b, j, pt: (b, 0, 0))
    grid_spec = pltpu.PrefetchScalarGridSpec(
        num_scalar_prefetch=1,
        grid=(bd, npages // npp),
        in_specs=[seq(s_new, d), seq(s_new, d), seq(s_new, d), seq(nh, LANES)]
                 + [pl.BlockSpec(memory_space=pl.ANY), pl.BlockSpec(memory_space=pl.ANY)]
                 + [paged(nh, page, i) for i in range(npp)],
        out_specs=seq(s_new, d),
        scratch_shapes=[pltpu.VMEM((d, nr), BF16), pltpu.VMEM((1, nr), F32), pltpu.VMEM((1, nr), F32),
                        pltpu.VMEM((1, nr), F32), pltpu.VMEM((nr, 2 * hd), F32), pltpu.VMEM((1, nh), F32),
                        pltpu.VMEM((2, npp, nh, page, hd), F32), pltpu.VMEM((2, npp, nh, page, hd), F32),
                        pltpu.SemaphoreType.DMA((2, 2))],
    )
    return pl.pallas_call(
        functools.partial(_fox_sample_kernel, nh=nh, hd=hd, npp=npp, scale=float(hd) ** -0.5,
                          layer=layer, npages=npages),
        grid_spec=grid_spec,
        out_shape=jax.ShapeDtypeStruct((bd, s_new, d), BF16),
        compiler_params=_cparams("arbitrary", "arbitrary"),
        name="fox_sample_attention",
    )(page_table, q, k_new, v_new, lf_new, cache_k, cache_v, *([cache_lf] * npp))


def kernel(x_prompt, x_sample, state_conv, state_C, state_n, state_m, cache_k, cache_v, cache_logf,
           page_table, meta_tokens, norm_mix_pre, norm_mix_post, norm_mlp_pre, norm_mlp_post,
           w_in_ab, conv_w, b_igate, b_fgate, mlstm_norm, w_out_ab, w_in_c, b_fox, w_out_c,
           w_mlp_up, w_mlp_down):
    bp, seq, d_model = x_prompt.shape
    bs, s_new, _ = x_sample.shape
    depth = norm_mix_pre.shape[0]
    n_meta = meta_tokens.shape[0]
    _, _, mh, hdb, _ = state_C.shape
    d_mlstm = mh * hdb
    d_conv = conv_w.shape[2]
    n_odd, n_pool, page, fh, fhd = cache_k.shape
    d_fox = fh * fhd
    tp = n_meta + seq

    meta = jnp.broadcast_to(meta_tokens[None], (bp, n_meta, d_model))
    xp = jnp.concatenate([meta, x_prompt], axis=1)
    xs = x_sample.reshape(1, bs * s_new, d_model)
    vec = lambda a: a.reshape(1, -1)

    clf = jnp.swapaxes(cache_logf, 2, 3)

    wt_ab = jnp.swapaxes(w_in_ab, 1, 2)
    wt_c = jnp.swapaxes(w_in_c, 1, 2)
    w_out_ab_b = w_out_ab.astype(BF16)
    w_out_c_b = w_out_c.astype(BF16)
    w_up_b = w_mlp_up.astype(BF16)
    w_down_b = w_mlp_down.astype(BF16)

    xpn = rmsnorm(xp, vec(norm_mix_pre[0]))
    xsn = rmsnorm(xs, vec(norm_mix_pre[0]))

    conv_p, c_p, n_p, m_p, conv_s, c_s, n_s, m_s = [], [], [], [], [], [], [], []
    k_p, v_p, lf_p, k_s, v_s, lf_s = [], [], [], [], [], []
    for layer in range(depth):
        if layer % 2 == 0:
            e = layer // 2
            n_main = 3 * d_conv + 4 * d_mlstm
            wg_t = wt_ab[e, n_main:, :]
            b_g = jnp.concatenate([b_igate[e], b_fgate[e]]).reshape(2 * mh, 1)
            nh_vec = vec(mlstm_norm[e])

            def mix(xn, prev, c0, n0, m0, chunk, nb):
                t = xn.shape[1]
                proj = matmul_cols(xn, wt_ab, e, 0, n_main, BF16)
                g = gates(xn, wg_t, b_g, mh)
                proj = proj.reshape(nb, t // (nb // xn.shape[0]), n_main)
                tl = proj.shape[1]
                g = g.reshape(xn.shape[0], 2, mh, nb // xn.shape[0], tl)
                gh = jnp.transpose(g, (0, 3, 2, 1, 4)).reshape(nb, mh, 2, tl)
                if tl < chunk:
                    gh = jnp.pad(gh, ((0, 0), (0, 0), (0, 0), (0, chunk - tl)))
                ya, cnew = conv_mixer(proj, prev, conv_w[e])
                yb, c1, n1, m1 = mlstm_mixer(proj, gh, c0, n0.reshape(nb, mh, 1, hdb), m0.reshape(nb, mh, 1, 1),
                                             nh_vec, 3 * d_conv, chunk)
                shp = (xn.shape[0], t, -1)
                return ya.reshape(shp), yb.reshape(shp), cnew, c1, n1.reshape(nb, mh, hdb), m1.reshape(nb, mh)

            zc = jnp.zeros((bp, mh, hdb, hdb), F32)
            ya, yb, cvp, c1, n1, m1 = mix(xpn, jnp.zeros((bp, 2, d_conv), F32), zc,
                                          jnp.zeros((bp, mh, hdb), F32), jnp.full((bp, mh), M_INIT, F32), 256, bp)
            conv_p.append(cvp); c_p.append(c1); n_p.append(n1); m_p.append(m1)
            xp = outproj_residual([ya, yb], w_out_ab_b, e, vec(norm_mix_post[layer]), xp)
            ya, yb, cvs, c1, n1, m1 = mix(xsn, state_conv[e], state_C[e], state_n[e], state_m[e], LANES, bs)
            conv_s.append(cvs); c_s.append(c1); n_s.append(n1); m_s.append(m1)
            xs = outproj_residual([ya, yb], w_out_ab_b, e, vec(norm_mix_post[layer]), xs)
        else:
            o = layer // 2
            wg_t = wt_c[o, 3 * d_fox:, :]
            b_g = b_fox[o].reshape(fh, 1)

            q = matmul_cols(xpn, wt_c, o, 0, d_fox, BF16)
            k, kb = matmul_cols(xpn, wt_c, o, d_fox, d_fox, F32, bf16_copy=True)
            v, vt = matmul_cols(xpn, wt_c, o, 2 * d_fox, d_fox, F32, tile=1408, transposed_copy=True)
            lf = gates(xpn, wg_t, b_g, 0)
            c_row, c_col = fox_cumsum(lf)
            att = fox_prompt_attention(q, kb, vt, c_row, c_col, fh)
            k_p.append(k.reshape(bp, tp, fh, fhd)); v_p.append(v.reshape(bp, tp, fh, fhd))
            lf_p.append(jnp.transpose(lf, (0, 2, 1)))
            xp = outproj_residual([att], w_out_c_b, o, vec(norm_mix_post[layer]), xp)

            q = matmul_cols(xsn, wt_c, o, 0, d_fox, BF16).reshape(bs, s_new, d_fox)
            k = matmul_cols(xsn, wt_c, o, d_fox, d_fox, F32).reshape(bs, s_new, d_fox)
            v = matmul_cols(xsn, wt_c, o, 2 * d_fox, d_fox, F32).reshape(bs, s_new, d_fox)
            lf = gates(xsn, wg_t, b_g, 0)
            lf = jnp.transpose(lf.reshape(fh, bs, s_new), (1, 0, 2))
            lf_pad = jnp.pad(lf, ((0, 0), (0, 0), (0, LANES - s_new)))
            att = fox_sample_attention(q, k, v, lf_pad, cache_k, cache_v, clf, page_table, o, fh)
            k_s.append(k.reshape(bs, s_new, fh, fhd)); v_s.append(v.reshape(bs, s_new, fh, fhd))
            lf_s.append(jnp.transpose(lf, (0, 2, 1)))
            xs = outproj_residual([att.reshape(1, bs * s_new, d_fox)], w_out_c_b, o, vec(norm_mix_post[layer]), xs)

        g_next = vec(norm_mix_pre[layer + 1]) if layer + 1 < depth else vec(norm_mix_pre[layer])
        xp, xpn = mlp_residual(xp, vec(norm_mlp_pre[layer]), w_up_b, w_down_b, layer, vec(norm_mlp_post[layer]), g_next)
        xs, xsn = mlp_residual(xs, vec(norm_mlp_pre[layer]), w_up_b, w_down_b, layer, vec(norm_mlp_post[layer]), g_next)

    y_prompt = xp[:, n_meta:]
    y_sample = xs.reshape(bs, s_new, d_model)
    return (y_prompt, y_sample,
            jnp.stack(conv_p), jnp.stack(c_p), jnp.stack(n_p), jnp.stack(m_p),
            jnp.stack(k_p), jnp.stack(v_p), jnp.stack(lf_p),
            jnp.stack(conv_s), jnp.stack(c_s), jnp.stack(n_s), jnp.stack(m_s),
            jnp.stack(k_s), jnp.stack(v_s), jnp.stack(lf_s))
```

```python
import functools

import jax
import jax.numpy as jnp
from jax import lax
from jax.experimental import pallas as pl
from jax.experimental.pallas import tpu as pltpu

F32 = jnp.float32
BF16 = jnp.bfloat16
EPS = 1e-6
M_INIT = -1e30
NEG = -1e30
HIGHEST = lax.Precision.HIGHEST
LOG2E = 1.4426950408889634

V7X_VMEM_BYTES = 64 * 1024 * 1024
VMEM_LIMIT = V7X_VMEM_BYTES - 8 * 1024 * 1024
LANES = 128
BF16_SUBLANES = 16

NT_DIMS = (((1,), (1,)), ((), ()))


def _cparams(*sem):
    return pltpu.CompilerParams(dimension_semantics=sem, vmem_limit_bytes=VMEM_LIMIT)


def _round_up(x, m):
    return -(-x // m) * m


def _row_tile(t, target, align=BF16_SUBLANES):
    if t <= target:
        return t
    n = pl.cdiv(t, target)
    return _round_up(pl.cdiv(t, n), align)


def _rms(x, g):
    ms = jnp.mean(x * x, axis=-1, keepdims=True)
    return x * lax.rsqrt(ms + EPS) * g


def _div_pow2(x, n):
    assert n & (n - 1) == 0
    return lax.shift_right_logical(x, n.bit_length() - 1)


def _mod_pow2(x, n):
    assert n & (n - 1) == 0
    return lax.bitwise_and(x, n - 1)


def _log_sigmoid(x):
    return jnp.minimum(x, 0.0) - jnp.log1p(jnp.exp(-jnp.abs(x)))


def _rmsnorm_kernel(x_ref, g_ref, o_ref):
    o_ref[...] = _rms(x_ref[...], g_ref[...]).astype(o_ref.dtype)


def rmsnorm(x, g, tile=1024):
    b, t, d = x.shape
    tm = _row_tile(t, tile)
    return pl.pallas_call(
        _rmsnorm_kernel,
        grid=(b, pl.cdiv(t, tm)),
        in_specs=[pl.BlockSpec((None, tm, d), lambda i, j: (i, j, 0)),
                  pl.BlockSpec((1, d), lambda i, j: (0, 0))],
        out_specs=pl.BlockSpec((None, tm, d), lambda i, j: (i, j, 0)),
        out_shape=jax.ShapeDtypeStruct((b, t, d), BF16),
        compiler_params=_cparams("parallel", "parallel"),
        name="rmsnorm",
    )(x, g)


def _matmul_kernel(x_ref, wt_ref, o_ref, *rest, bf16_copy, transposed_copy):
    wb_ref = rest[-1]

    @pl.when((pl.program_id(1) == 0) & (pl.program_id(2) == 0))
    def _():
        wb_ref[...] = wt_ref[...].astype(BF16)

    y = lax.dot_general(x_ref[...], wb_ref[...], NT_DIMS, preferred_element_type=F32)
    o_ref[...] = y.astype(o_ref.dtype)
    if bf16_copy:
        rest[0][...] = y.astype(BF16)
    if transposed_copy:
        rest[0][...] = y.T.astype(BF16)


def matmul_cols(x, wt, layer, col0, ncols, out_dtype, tn=1024, tile=1376, bf16_copy=False, transposed_copy=False):
    b, t, k = x.shape
    assert not (bf16_copy and transposed_copy)
    tm = _row_tile(t, tile, LANES if transposed_copy else BF16_SUBLANES)
    assert col0 % tn == 0 and ncols % tn == 0
    c0 = col0 // tn
    out_specs = [pl.BlockSpec((None, tm, tn), lambda n, i, j: (i, j, n))]
    out_shape = [jax.ShapeDtypeStruct((b, t, ncols), out_dtype)]
    if bf16_copy:
        out_specs.append(out_specs[0])
        out_shape.append(jax.ShapeDtypeStruct((b, t, ncols), BF16))
    if transposed_copy:
        assert tm % LANES == 0
        out_specs.append(pl.BlockSpec((None, tn, tm), lambda n, i, j: (i, n, j)))
        out_shape.append(jax.ShapeDtypeStruct((b, ncols, t), BF16))
    res = pl.pallas_call(
        functools.partial(_matmul_kernel, bf16_copy=bf16_copy, transposed_copy=transposed_copy),
        grid=(ncols // tn, b, pl.cdiv(t, tm)),
        in_specs=[pl.BlockSpec((None, tm, k), lambda n, i, j: (i, j, 0)),
                  pl.BlockSpec((None, tn, k), lambda n, i, j: (layer, c0 + n, 0))],
        out_specs=out_specs,
        out_shape=out_shape,
        scratch_shapes=[pltpu.VMEM((tn, k), BF16)],
        compiler_params=_cparams("arbitrary", "arbitrary", "arbitrary"),
        name="matmul_cols",
    )(x, wt)
    return res if (bf16_copy or transposed_copy) else res[0]


def _gates_kernel(x_ref, wt_ref, b_ref, o_ref, *, n_plain):
    g = lax.dot_general(wt_ref[...].astype(BF16), x_ref[...], NT_DIMS, preferred_element_type=F32)
    g = g + b_ref[...]
    row = lax.broadcasted_iota(jnp.int32, g.shape, 0)
    o_ref[...] = jnp.where(row < n_plain, g, _log_sigmoid(g))


def gates(x, wt, bias, n_plain, tile=1024):
    b, t, k = x.shape
    ng = wt.shape[0]
    tm = t if t <= tile else tile
    return pl.pallas_call(
        functools.partial(_gates_kernel, n_plain=n_plain),
        grid=(b, pl.cdiv(t, tm)),
        in_specs=[pl.BlockSpec((None, tm, k), lambda i, j: (i, j, 0)),
                  pl.BlockSpec((ng, k), lambda i, j: (0, 0)),
                  pl.BlockSpec((ng, 1), lambda i, j: (0, 0))],
        out_specs=pl.BlockSpec((None, ng, tm), lambda i, j: (i, 0, j)),
        out_shape=jax.ShapeDtypeStruct((b, ng, t), F32),
        compiler_params=_cparams("parallel", "parallel"),
        name="gates",
    )(x, wt, bias)


def _conv_kernel(ab_ref, ac_ref, ax_ref, prev_ref, w_ref, y_ref, new_ref, carry_ref, *, tail):
    j = pl.program_id(1)

    @pl.when(j == 0)
    def _():
        carry_ref[...] = prev_ref[...]

    u = ac_ref[...].astype(F32) * ax_ref[...].astype(F32)
    tc = u.shape[0]
    row = lax.broadcasted_iota(jnp.int32, u.shape, 0)
    c0 = carry_ref[0:1, :]
    c1 = carry_ref[1:2, :]
    u1 = jnp.where(row == 0, c1, pltpu.roll(u, 1, axis=0))
    u2 = jnp.where(row == 0, c0, jnp.where(row == 1, c1, pltpu.roll(u, 2, axis=0)))
    z = w_ref[0:1, :] * u2 + w_ref[1:2, :] * u1 + w_ref[2:3, :] * u
    y_ref[...] = (ab_ref[...].astype(F32) * z).astype(y_ref.dtype)
    carry_ref[...] = u[tc - 2:tc, :]

    @pl.when(j == pl.num_programs(1) - 1)
    def _():
        new_ref[...] = u[tail:tail + 2, :]


def conv_mixer(p, prev, w, tile=688):
    b, t, _ = p.shape
    c = w.shape[1]
    tc = _row_tile(t, tile)
    nt = pl.cdiv(t, tc)
    tail = (t - 2) - (nt - 1) * tc
    assert 0 <= tail and tail + 2 <= tc
    return pl.pallas_call(
        functools.partial(_conv_kernel, tail=tail),
        grid=(b, nt),
        in_specs=[pl.BlockSpec((None, tc, c), lambda i, j: (i, j, 0)),
                  pl.BlockSpec((None, tc, c), lambda i, j: (i, j, 1)),
                  pl.BlockSpec((None, tc, c), lambda i, j: (i, j, 2)),
                  pl.BlockSpec((None, 2, c), lambda i, j: (i, 0, 0)),
                  pl.BlockSpec((3, c), lambda i, j: (0, 0))],
        out_specs=[pl.BlockSpec((None, tc, c), lambda i, j: (i, j, 0)),
                   pl.BlockSpec((None, 2, c), lambda i, j: (i, 0, 0))],
        out_shape=[jax.ShapeDtypeStruct((b, t, c), BF16),
                   jax.ShapeDtypeStruct((b, 2, c), F32)],
        scratch_shapes=[pltpu.VMEM((2, c), F32)],
        compiler_params=_cparams("parallel", "arbitrary"),
        name="conv_mixer",
    )(p, p, p, prev, w)


def _mlstm_kernel(q_ref, k_ref, v_ref, o_ref, g_ref, c0_ref, n0_ref, m0_ref, nh_ref,
                  y_ref, cout_ref, nout_ref, mout_ref, c_s, n_s, m_s, *, chunk, total, scale):
    ci = pl.program_id(1)

    @pl.when(ci == 0)
    def _():
        c_s[...] = c0_ref[...]
        n_s[...] = n0_ref[...]
        m_s[...] = m0_ref[...]

    lb = q_ref.shape[0]
    nheads, dv, dk = c_s.shape
    valid = total - ci * chunk
    rows = lax.broadcasted_iota(jnp.int32, (chunk, 1), 0)
    lane = lax.broadcasted_iota(jnp.int32, (1, chunk), 1)
    ri = lax.broadcasted_iota(jnp.int32, (chunk, chunk), 0)
    cj = lax.broadcasted_iota(jnp.int32, (chunk, chunk), 1)
    causal = cj <= ri
    eye = cj == ri

    def load(ref, sl):
        x = ref[:, sl].astype(F32)
        if lb < chunk:
            x = jnp.concatenate([x, jnp.zeros((chunk - lb, x.shape[1]), F32)], axis=0)
        return jnp.where(rows < valid, x, 0.0)

    for h in range(nheads):
        sk = slice(h * dk, (h + 1) * dk)
        sv = slice(h * dv, (h + 1) * dv)
        q = load(q_ref, sk)
        k = load(k_ref, sk) * scale
        v = load(v_ref, sv)
        qb = q.astype(BF16)
        kb = k.astype(BF16)

        li = jnp.where(lane < valid, g_ref[h, 0:1, :], NEG)
        lf = jnp.where(lane < valid, g_ref[h, 1:2, :], 0.0)
        b_col = jnp.sum(jnp.where(causal, lf, 0.0), axis=1, keepdims=True)
        b_row = jnp.sum(jnp.where(eye, b_col, 0.0), axis=0, keepdims=True)
        li_col = jnp.sum(jnp.where(eye, li, 0.0), axis=1, keepdims=True)

        m_prev = m_s[h]
        c_prev = c_s[h]
        n_prev = n_s[h]
        d = jnp.where(causal, b_col - b_row + li, NEG)
        g_col = m_prev + b_col
        m_t = jnp.maximum(g_col, jnp.max(d, axis=1, keepdims=True))
        s = lax.dot_general(qb, kb, NT_DIMS, preferred_element_type=F32) * jnp.exp(d - m_t)
        inter = jnp.exp(g_col - m_t)
        cq = lax.dot_general(qb, c_prev.astype(BF16), NT_DIMS, preferred_element_type=F32)
        num = jnp.dot(s.astype(BF16), v.astype(BF16), preferred_element_type=F32) + inter * cq
        den = jnp.sum(s, axis=1, keepdims=True) + inter * jnp.sum(q * n_prev, axis=1, keepdims=True)
        hh = num / jnp.maximum(jnp.abs(den), jnp.exp(-m_t))

        m_new = m_t[chunk - 1:chunk, :]
        b_last = b_row[:, chunk - 1:chunk]
        w_col = jnp.exp(b_last - b_col + li_col - m_new)
        decay = jnp.exp(m_prev + b_last - m_new)
        vw = (v * w_col).T.astype(BF16)
        c_s[h] = decay * c_prev + jnp.dot(vw, kb, preferred_element_type=F32)
        n_s[h] = decay * n_prev + jnp.sum(w_col * k, axis=0, keepdims=True)
        m_s[h] = m_new

        mu = jnp.mean(hh, axis=1, keepdims=True)
        hc = hh - mu
        var = jnp.mean(hc * hc, axis=1, keepdims=True)
        hn = hc * lax.rsqrt(var + EPS) * nh_ref[:, sv]
        y = jax.nn.sigmoid(o_ref[:, sv].astype(F32)) * hn[:lb]
        y_ref[:, sv] = y.astype(y_ref.dtype)

    @pl.when(ci == pl.num_programs(1) - 1)
    def _():
        cout_ref[...] = c_s[...]
        nout_ref[...] = n_s[...]
        mout_ref[...] = m_s[...]


def mlstm_mixer(p, gh, c0, n0, m0, norm_h, col0, chunk):
    b, t, _ = p.shape
    _, h, dv, dk = c0.shape
    assert dv == dk and col0 % (h * dk) == 0
    lb = min(chunk, t)
    nc = pl.cdiv(t, chunk)
    blk0 = col0 // (h * dk)

    def pspec(group):
        return pl.BlockSpec((None, lb, h * dk), lambda i, c: (i, c, blk0 + group))

    state4 = lambda r, cdim: pl.BlockSpec((None, h, r, cdim), lambda i, c: (i, 0, 0, 0))
    return pl.pallas_call(
        functools.partial(_mlstm_kernel, chunk=chunk, total=t, scale=float(dk) ** -0.5),
        grid=(b, nc),
        in_specs=[pspec(0), pspec(1), pspec(2), pspec(3),
                  pl.BlockSpec((None, h, 2, chunk), lambda i, c: (i, 0, 0, c)),
                  state4(dv, dk), state4(1, dk), state4(1, 1),
                  pl.BlockSpec((1, h * dv), lambda i, c: (0, 0))],
        out_specs=[pl.BlockSpec((None, lb, h * dv), lambda i, c: (i, c, 0)),
                   state4(dv, dk), state4(1, dk), state4(1, 1)],
        out_shape=[jax.ShapeDtypeStruct((b, t, h * dv), BF16),
                   jax.ShapeDtypeStruct((b, h, dv, dk), F32),
                   jax.ShapeDtypeStruct((b, h, 1, dk), F32),
                   jax.ShapeDtypeStruct((b, h, 1, 1), F32)],
        scratch_shapes=[pltpu.VMEM((h, dv, dk), F32), pltpu.VMEM((h, 1, dk), F32), pltpu.VMEM((h, 1, 1), F32)],
        compiler_params=_cparams("parallel", "arbitrary"),
        name="mlstm_mixer",
    )(p, p, p, p, gh, c0, n0, m0, norm_h)


def _outproj_kernel(*refs, n_in):
    a_refs = refs[:n_in]
    w_ref, g_ref, x_ref, o_ref = refs[n_in:]
    y = None
    k0 = 0
    for a_ref in a_refs:
        kk = a_ref.shape[1]
        part = jnp.dot(a_ref[...], w_ref[k0:k0 + kk, :], preferred_element_type=F32)
        y = part if y is None else y + part
        k0 += kk
    o_ref[...] = x_ref[...] + _rms(y, g_ref[...])


def outproj_residual(a_list, w, layer, g, x, tile=688):
    b, t, d = x.shape
    tm = _row_tile(t, tile)
    kt = w.shape[1]
    a_specs = [pl.BlockSpec((None, tm, a.shape[2]), lambda i, j: (i, j, 0)) for a in a_list]
    return pl.pallas_call(
        functools.partial(_outproj_kernel, n_in=len(a_list)),
        grid=(b, pl.cdiv(t, tm)),
        in_specs=a_specs + [pl.BlockSpec((None, kt, d), lambda i, j: (layer, 0, 0)),
                            pl.BlockSpec((1, d), lambda i, j: (0, 0)),
                            pl.BlockSpec((None, tm, d), lambda i, j: (i, j, 0))],
        out_specs=pl.BlockSpec((None, tm, d), lambda i, j: (i, j, 0)),
        out_shape=jax.ShapeDtypeStruct((b, t, d), F32),
        compiler_params=_cparams("parallel", "parallel"),
        name="outproj_residual",
    )(*a_list, w, g, x)


def _mlp_kernel(x_ref, g1_ref, wu_ref, wd_ref, g2_ref, g3_ref, o_ref, on_ref, xn_s, acc_s):
    f = pl.program_id(2)

    @pl.when(f == 0)
    def _():
        xn_s[...] = _rms(x_ref[...], g1_ref[...]).astype(BF16)
        acc_s[...] = jnp.zeros_like(acc_s)

    h = jnp.maximum(jnp.dot(xn_s[...], wu_ref[...], preferred_element_type=F32), 0.0)
    acc_s[...] += jnp.dot((h * h).astype(BF16), wd_ref[...], preferred_element_type=F32)

    @pl.when(f == pl.num_programs(2) - 1)
    def _():
        xnew = x_ref[...] + _rms(acc_s[...], g2_ref[...])
        o_ref[...] = xnew
        on_ref[...] = _rms(xnew, g3_ref[...]).astype(on_ref.dtype)


def mlp_residual(x, g_pre, w_up, w_down, layer, g_post, g_next, tile=688, tf=1024):
    b, t, d = x.shape
    f = w_up.shape[2]
    tm = _row_tile(t, tile)
    row = pl.BlockSpec((None, tm, d), lambda i, j, l: (i, j, 0))
    vec = pl.BlockSpec((1, d), lambda i, j, l: (0, 0))
    return pl.pallas_call(
        _mlp_kernel,
        grid=(b, pl.cdiv(t, tm), f // tf),
        in_specs=[row, vec,
                  pl.BlockSpec((None, d, tf), lambda i, j, l: (layer, 0, l)),
                  pl.BlockSpec((None, tf, d), lambda i, j, l: (layer, l, 0)),
                  vec, vec],
        out_specs=[row, row],
        out_shape=[jax.ShapeDtypeStruct((b, t, d), F32), jax.ShapeDtypeStruct((b, t, d), BF16)],
        scratch_shapes=[pltpu.VMEM((tm, d), BF16), pltpu.VMEM((tm, d), F32)],
        compiler_params=_cparams("parallel", "parallel", "arbitrary"),
        name="mlp_residual",
    )(x, g_pre, w_up, w_down, g_post, g_next)


def _fox_cumsum_kernel(lf_ref, crow_ref, ccol_ref, carry_ref):
    j = pl.program_id(1)

    @pl.when(j == 0)
    def _():
        carry_ref[...] = jnp.zeros_like(carry_ref)

    nh, tc = lf_ref.shape
    ri = lax.broadcasted_iota(jnp.int32, (tc, tc), 0)
    cj = lax.broadcasted_iota(jnp.int32, (tc, tc), 1)
    causal = cj <= ri
    eye = cj == ri
    lane = lax.broadcasted_iota(jnp.int32, (tc, LANES), 1)
    cols = jnp.zeros((tc, LANES), F32)
    for h in range(nh):
        col = jnp.sum(jnp.where(causal, lf_ref[h:h + 1, :], 0.0), axis=1, keepdims=True)
        col = col + carry_ref[h:h + 1, :]
        crow_ref[h:h + 1, :] = jnp.sum(jnp.where(eye, col, 0.0), axis=0, keepdims=True) * LOG2E
        carry_ref[h:h + 1, :] = col[tc - 1:tc, :]
        cols = jnp.where(lane == h, col, cols)
    ccol_ref[...] = cols * LOG2E


def fox_cumsum(lf, tile=512):
    b, nh, t = lf.shape
    tc = tile
    return pl.pallas_call(
        _fox_cumsum_kernel,
        grid=(b, pl.cdiv(t, tc)),
        in_specs=[pl.BlockSpec((None, nh, tc), lambda i, j: (i, 0, j))],
        out_specs=[pl.BlockSpec((None, nh, tc), lambda i, j: (i, 0, j)),
                   pl.BlockSpec((None, tc, LANES), lambda i, j: (i, j, 0))],
        out_shape=[jax.ShapeDtypeStruct((b, nh, t), F32), jax.ShapeDtypeStruct((b, t, LANES), F32)],
        scratch_shapes=[pltpu.VMEM((nh, 1), F32)],
        compiler_params=_cparams("parallel", "arbitrary"),
        name="fox_cumsum",
    )(lf)


def _fox_prompt_kernel(q_ref, k_ref, vt_ref, cq_ref, ck_ref, o_ref, m_s, l_s, acc_s, *, nh, hd, total, scale):
    qi = pl.program_id(1)
    ki = pl.program_id(2)
    tq = q_ref.shape[0]
    tk = k_ref.shape[0]

    @pl.when(ki == 0)
    def _():
        m_s[...] = jnp.full_like(m_s, NEG)
        l_s[...] = jnp.zeros_like(l_s)
        acc_s[...] = jnp.zeros_like(acc_s)

    def step(diagonal):
        if diagonal:
            kpos = ki * tk + lax.broadcasted_iota(jnp.int32, (tk, tq), 0)
            qpos = qi * tq + lax.broadcasted_iota(jnp.int32, (tk, tq), 1)
            visible = kpos <= qpos
            vcol_ok = (ki * tk + lax.broadcasted_iota(jnp.int32, (1, tk), 1)) < total
        for h in range(nh):
            sl = slice(h * hd, (h + 1) * hd)
            st = lax.dot_general(k_ref[:, sl], q_ref[:, sl], NT_DIMS, preferred_element_type=F32)
            st = st * (scale * LOG2E) + cq_ref[h:h + 1, :] - ck_ref[:, h:h + 1]
            vt = vt_ref[sl, :]
            if diagonal:
                st = jnp.where(visible, st, NEG)
                vt = jnp.where(vcol_ok, vt, jnp.zeros_like(vt))
            m_prev = m_s[h:h + 1, :]
            m_new = jnp.maximum(m_prev, jnp.max(st, axis=0, keepdims=True))
            alpha = jnp.exp2(m_prev - m_new)
            p = jnp.exp2(st - m_new)
            l_s[h:h + 1, :] = alpha * l_s[h:h + 1, :] + jnp.sum(p, axis=0, keepdims=True)
            acc_s[sl, :] = alpha * acc_s[sl, :] + jnp.dot(vt, p.astype(BF16), preferred_element_type=F32)
            m_s[h:h + 1, :] = m_new

    @pl.when(ki < qi)
    def _():
        step(False)

    @pl.when(ki == qi)
    def _():
        step(True)
        for h in range(nh):
            sl = slice(h * hd, (h + 1) * hd)
            o_ref[:, sl] = (acc_s[sl, :] / l_s[h:h + 1, :]).T.astype(o_ref.dtype)


def fox_prompt_attention(q, k, vt, c_row, c_col, nh, tile=512):
    b, t, d = q.shape
    hd = d // nh
    tq = tk = tile
    nq = pl.cdiv(t, tq)
    kmap = lambda i, a, c: (i, jnp.minimum(c, a), 0)
    return pl.pallas_call(
        functools.partial(_fox_prompt_kernel, nh=nh, hd=hd, total=t, scale=float(hd) ** -0.5),
        grid=(b, nq, nq),
        in_specs=[pl.BlockSpec((None, tq, d), lambda i, a, c: (i, a, 0)),
                  pl.BlockSpec((None, tk, d), kmap),
                  pl.BlockSpec((None, d, tk), lambda i, a, c: (i, 0, jnp.minimum(c, a))),
                  pl.BlockSpec((None, nh, tq), lambda i, a, c: (i, 0, a)),
                  pl.BlockSpec((None, tk, LANES), kmap)],
        out_specs=pl.BlockSpec((None, tq, d), lambda i, a, c: (i, a, 0)),
        out_shape=jax.ShapeDtypeStruct((b, t, d), BF16),
        scratch_shapes=[pltpu.VMEM((nh, tq), F32), pltpu.VMEM((nh, tq), F32), pltpu.VMEM((d, tq), F32)],
        compiler_params=_cparams("parallel", "parallel", "arbitrary"),
        name="fox_prompt_attention",
    )(q, k, vt, c_row, c_col)


def _fox_sample_kernel(pt_ref, q_ref, kn_ref, vn_ref, lfn_ref, ck_hbm, cv_hbm, *refs,
                       nh, hd, npp, scale, layer, npages):
    lf_refs = refs[:npp]
    o_ref, qbdt_s, crow_s, m_s, l_s, acc_s, carry_s, kbuf, vbuf, sem = refs[npp:]
    b = pl.program_id(0)
    j = pl.program_id(1)
    nb = pl.num_programs(0)
    nj = pl.num_programs(1)
    slot = lax.rem(j, 2)

    def page_copies(bb, jj, sl):
        cps = []
        for i in range(npp):
            phys = pt_ref[bb, npages - 1 - (jj * npp + i)]
            for h in range(nh):
                cps.append(pltpu.make_async_copy(ck_hbm.at[layer, phys, :, h, :], kbuf.at[sl, i, h], sem.at[sl, 0]))
                cps.append(pltpu.make_async_copy(cv_hbm.at[layer, phys, :, h, :], vbuf.at[sl, i, h], sem.at[sl, 1]))
        return cps

    @pl.when((b == 0) & (j == 0))
    def _():
        for cp in page_copies(b, j, slot):
            cp.start()

    is_last = (b == nb - 1) & (j == nj - 1)
    nxt_j = jnp.where(j == nj - 1, 0, j + 1)
    nxt_b = jnp.where(j == nj - 1, b + 1, b)

    @pl.when(jnp.logical_not(is_last))
    def _():
        for cp in page_copies(nxt_b, nxt_j, 1 - slot):
            cp.start()

    for cp in page_copies(b, j, slot):
        cp.wait()
    s_new = q_ref.shape[0]
    page = lf_refs[0].shape[1]
    nr = nh * s_new
    d = nh * hd
    npair = nh // 2

    expand_t = (lax.broadcasted_iota(jnp.int32, (nh, nr), 0)
                == _div_pow2(lax.broadcasted_iota(jnp.int32, (nh, nr), 1), s_new)).astype(F32)
    eye_r = lax.broadcasted_iota(jnp.int32, (nr, nr), 0) == lax.broadcasted_iota(jnp.int32, (nr, nr), 1)

    def update(st, v_pairs):
        m_prev = m_s[...]
        m_new = jnp.maximum(m_prev, jnp.max(st, axis=0, keepdims=True))
        alpha = jnp.exp(m_prev - m_new)
        pt = jnp.exp(st - m_new)
        l_s[...] = alpha * l_s[...] + jnp.sum(pt, axis=0, keepdims=True)
        m_s[...] = m_new
        p = pt.T.astype(BF16)
        alpha_col = jnp.sum(jnp.where(eye_r, alpha, 0.0), axis=1, keepdims=True)
        for c in range(npair):
            rs = slice(2 * s_new * c, 2 * s_new * (c + 1))
            acc_s[rs, :] = alpha_col[rs, :] * acc_s[rs, :] + jnp.dot(p[rs, :], v_pairs[c], preferred_element_type=F32)

    @pl.when(j == 0)
    def _():
        qrep = jnp.concatenate([q_ref[...].astype(F32)] * nh, axis=0)
        rh = _div_pow2(lax.broadcasted_iota(jnp.int32, (nr, d), 0), s_new)
        ch = _div_pow2(lax.broadcasted_iota(jnp.int32, (nr, d), 1), hd)
        qbdt_s[...] = jnp.where(rh == ch, qrep, 0.0).T.astype(BF16)
        carry_s[...] = jnp.zeros_like(carry_s)
        m_s[...] = jnp.full_like(m_s, NEG)
        l_s[...] = jnp.zeros_like(l_s)
        acc_s[...] = jnp.zeros_like(acc_s)

        si = lax.broadcasted_iota(jnp.int32, (LANES, LANES), 0)
        sj = lax.broadcasted_iota(jnp.int32, (LANES, LANES), 1)
        lower = (sj <= si).astype(F32)
        c_new_t = lax.dot_general(lower, lfn_ref[...], NT_DIMS, preferred_element_type=F32, precision=HIGHEST)
        c_b = jnp.dot(c_new_t, expand_t, preferred_element_type=F32, precision=HIGHEST)
        key = lax.broadcasted_iota(jnp.int32, (LANES, nr), 0)
        tok = _mod_pow2(lax.broadcasted_iota(jnp.int32, (LANES, nr), 1), s_new)
        crow = jnp.sum(jnp.where(key == tok, c_b, 0.0), axis=0, keepdims=True)
        crow_s[...] = crow
        pad = jnp.zeros((LANES - s_new, d), F32)
        kn = jnp.concatenate([kn_ref[...], pad], axis=0).astype(BF16)
        vn = jnp.concatenate([vn_ref[...], pad], axis=0).astype(BF16)
        st = jnp.dot(kn, qbdt_s[...], preferred_element_type=F32) * scale
        st = jnp.where((key < s_new) & (key <= tok), st + crow - c_b, NEG)
        update(st, [vn[:, 2 * hd * c:2 * hd * (c + 1)] for c in range(npair)])

    pi = lax.broadcasted_iota(jnp.int32, (page, page), 0)
    pj = lax.broadcasted_iota(jnp.int32, (page, page), 1)
    later = (pj > pi).astype(F32)
    ones = jnp.ones((8, page), F32)
    st_parts = []
    for i in range(npp):
        lfp = lf_refs[i][...]
        rev = lax.dot_general(later, lfp, NT_DIMS, preferred_element_type=F32, precision=HIGHEST) + carry_s[...]
        carry_s[...] = carry_s[...] + lax.dot_general(ones, lfp, NT_DIMS, preferred_element_type=F32,
                                                      precision=HIGHEST)[0:1, :]
        bias = jnp.dot(rev, expand_t, preferred_element_type=F32, precision=HIGHEST)
        kb = jnp.concatenate([kbuf[slot, i, h].astype(BF16) for h in range(nh)], axis=1)
        st = jnp.dot(kb, qbdt_s[...], preferred_element_type=F32) * scale
        st_parts.append(st + crow_s[...] + bias)
    v_pairs = []
    for c in range(npair):
        v_pairs.append(jnp.concatenate(
            [jnp.concatenate([vbuf[slot, i, 2 * c].astype(BF16),
                              vbuf[slot, i, 2 * c + 1].astype(BF16)], axis=1) for i in range(npp)], axis=0))
    update(jnp.concatenate(st_parts, axis=0), v_pairs)

    @pl.when(j == pl.num_programs(1) - 1)
    def _():
        l_col = jnp.sum(jnp.where(eye_r, l_s[...], 0.0), axis=1, keepdims=True)
        for h in range(nh):
            rs = slice(h * s_new, (h + 1) * s_new)
            cs = slice((h % 2) * hd, (h % 2 + 1) * hd)
            o_ref[:, h * hd:(h + 1) * hd] = (acc_s[rs, cs] / l_col[rs, :]).astype(o_ref.dtype)


def fox_sample_attention(q, k_new, v_new, lf_new, cache_k, cache_v, cache_lf, page_table, layer, nh, max_npp=8):
    bd, s_new, d = q.shape
    hd = d // nh
    page = cache_lf.shape[3]
    npages = page_table.shape[1]
    npp = max(n for n in range(1, max_npp + 1) if npages % n == 0 and (npages // n) % 2 == 0)
    assert nh * s_new == LANES and page == LANES and nh % 2 == 0
    nr = nh * s_new

    def paged(rows, width, i):
        return pl.BlockSpec((None, None, rows, width),
                            lambda b, j, pt: (layer, pt[b, npages - 1 - (j * npp + i)], 0, 0))

    seq = lambda r, w: pl.BlockSpec((None, r, w), lambda b, j, pt: (b, 0, 0))
    grid_spec = pltpu.PrefetchScalarGridSpec(
        num_scalar_prefetch=1,
        grid=(bd, npages // npp),
        in_specs=[seq(s_new, d), seq(s_new, d), seq(s_new, d), seq(nh, LANES)]
                 + [pl.BlockSpec(memory_space=pl.ANY), pl.BlockSpec(memory_space=pl.ANY)]
                 + [paged(nh, page, i) for i in range(npp)],
        out_specs=seq(s_new, d),
        scratch_shapes=[pltpu.VMEM((d, nr), BF16), pltpu.VMEM((1, nr), F32), pltpu.VMEM((1, nr), F32),
                        pltpu.VMEM((1, nr), F32), pltpu.VMEM((nr, 2 * hd), F32), pltpu.VMEM((1, nh), F32),
                        pltpu.VMEM((2, npp, nh, page, hd), F32), pltpu.VMEM((2, npp, nh, page, hd), F32),
                        pltpu.SemaphoreType.DMA((2, 2))],
    )
    return pl.pallas_call(
        functools.partial(_fox_sample_kernel, nh=nh, hd=hd, npp=npp, scale=float(hd) ** -0.5,
                          layer=layer, npages=npages),
        grid_spec=grid_spec,
        out_shape=jax.ShapeDtypeStruct((bd, s_new, d), BF16),
        compiler_params=_cparams("arbitrary", "arbitrary"),
        name="fox_sample_attention",
    )(page_table, q, k_new, v_new, lf_new, cache_k, cache_v, *([cache_lf] * npp))


def kernel(x_prompt, x_sample, state_conv, state_C, state_n, state_m, cache_k, cache_v, cache_logf,
           page_table, meta_tokens, norm_mix_pre, norm_mix_post, norm_mlp_pre, norm_mlp_post,
           w_in_ab, conv_w, b_igate, b_fgate, mlstm_norm, w_out_ab, w_in_c, b_fox, w_out_c,
           w_mlp_up, w_mlp_down):
    bp, seq, d_model = x_prompt.shape
    bs, s_new, _ = x_sample.shape
    depth = norm_mix_pre.shape[0]
    n_meta = meta_tokens.shape[0]
    _, _, mh, hdb, _ = state_C.shape
    d_mlstm = mh * hdb
    d_conv = conv_w.shape[2]
    n_odd, n_pool, page, fh, fhd = cache_k.shape
    d_fox = fh * fhd
    tp = n_meta + seq

    meta = jnp.broadcast_to(meta_tokens[None], (bp, n_meta, d_model))
    xp = jnp.concatenate([meta, x_prompt], axis=1)
    xs = x_sample.reshape(1, bs * s_new, d_model)
    vec = lambda a: a.reshape(1, -1)

    clf = jnp.swapaxes(cache_logf, 2, 3)

    wt_ab = jnp.swapaxes(w_in_ab, 1, 2)
    wt_c = jnp.swapaxes(w_in_c, 1, 2)
    w_out_ab_b = w_out_ab.astype(BF16)
    w_out_c_b = w_out_c.astype(BF16)
    w_up_b = w_mlp_up.astype(BF16)
    w_down_b = w_mlp_down.astype(BF16)

    xpn = rmsnorm(xp, vec(norm_mix_pre[0]))
    xsn = rmsnorm(xs, vec(norm_mix_pre[0]))

    conv_p, c_p, n_p, m_p, conv_s, c_s, n_s, m_s = [], [], [], [], [], [], [], []
    k_p, v_p, lf_p, k_s, v_s, lf_s = [], [], [], [], [], []
    for layer in range(depth):
        if layer % 2 == 0:
            e = layer // 2
            n_main = 3 * d_conv + 4 * d_mlstm
            wg_t = wt_ab[e, n_main:, :]
            b_g = jnp.concatenate([b_igate[e], b_fgate[e]]).reshape(2 * mh, 1)
            nh_vec = vec(mlstm_norm[e])

            def mix(xn, prev, c0, n0, m0, chunk, nb):
                t = xn.shape[1]
                proj = matmul_cols(xn, wt_ab, e, 0, n_main, BF16)
                g = gates(xn, wg_t, b_g, mh)
                proj = proj.reshape(nb, t // (nb // xn.shape[0]), n_main)
                tl = proj.shape[1]
                g = g.reshape(xn.shape[0], 2, mh, nb // xn.shape[0], tl)
                gh = jnp.transpose(g, (0, 3, 2, 1, 4)).reshape(nb, mh, 2, tl)
                if tl < chunk:
                    gh = jnp.pad(gh, ((0, 0), (0, 0), (0, 0), (0, chunk - tl)))
                ya, cnew = conv_mixer(proj, prev, conv_w[e])
                yb, c1, n1, m1 = mlstm_mixer(proj, gh, c0, n0.reshape(nb, mh, 1, hdb), m0.reshape(nb, mh, 1, 1),
                                             nh_vec, 3 * d_conv, chunk)
                shp = (xn.shape[0], t, -1)
                return ya.reshape(shp), yb.reshape(shp), cnew, c1, n1.reshape(nb, mh, hdb), m1.reshape(nb, mh)

            zc = jnp.zeros((bp, mh, hdb, hdb), F32)
            ya, yb, cvp, c1, n1, m1 = mix(xpn, jnp.zeros((bp, 2, d_conv), F32), zc,
                                          jnp.zeros((bp, mh, hdb), F32), jnp.full((bp, mh), M_INIT, F32), 256, bp)
            conv_p.append(cvp); c_p.append(c1); n_p.append(n1); m_p.append(m1)
            xp = outproj_residual([ya, yb], w_out_ab_b, e, vec(norm_mix_post[layer]), xp)
            ya, yb, cvs, c1, n1, m1 = mix(xsn, state_conv[e], state_C[e], state_n[e], state_m[e], LANES, bs)
            conv_s.append(cvs); c_s.append(c1); n_s.append(n1); m_s.append(m1)
            xs = outproj_residual([ya, yb], w_out_ab_b, e, vec(norm_mix_post[layer]), xs)
        else:
            o = layer // 2
            wg_t = wt_c[o, 3 * d_fox:, :]
            b_g = b_fox[o].reshape(fh, 1)

            q = matmul_cols(xpn, wt_c, o, 0, d_fox, BF16)
            k, kb = matmul_cols(xpn, wt_c, o, d_fox, d_fox, F32, bf16_copy=True)
            v, vt = matmul_cols(xpn, wt_c, o, 2 * d_fox, d_fox, F32, tile=1408, transposed_copy=True)
            lf = gates(xpn, wg_t, b_g, 0)
            c_row, c_col = fox_cumsum(lf)
            att = fox_prompt_attention(q, kb, vt, c_row, c_col, fh)
            k_p.append(k.reshape(bp, tp, fh, fhd)); v_p.append(v.reshape(bp, tp, fh, fhd))
            lf_p.append(jnp.transpose(lf, (0, 2, 1)))
            xp = outproj_residual([att], w_out_c_b, o, vec(norm_mix_post[layer]), xp)

            q = matmul_cols(xsn, wt_c, o, 0, d_fox, BF16).reshape(bs, s_new, d_fox)
            k = matmul_cols(xsn, wt_c, o, d_fox, d_fox, F32).reshape(bs, s_new, d_fox)
            v = matmul_cols(xsn, wt_c, o, 2 * d_fox, d_fox, F32).reshape(bs, s_new, d_fox)
            lf = gates(xsn, wg_t, b_g, 0)
            lf = jnp.transpose(lf.reshape(fh, bs, s_new), (1, 0, 2))
            lf_pad = jnp.pad(lf, ((0, 0), (0, 0), (0, LANES - s_new)))
            att = fox_sample_attention(q, k, v, lf_pad, cache_k, cache_v, clf, page_table, o, fh)
            k_s.append(k.reshape(bs, s_new, fh, fhd)); v_s.append(v.reshape(bs, s_new, fh, fhd))
            lf_s.append(jnp.transpose(lf, (0, 2, 1)))
            xs = outproj_residual([att.reshape(1, bs * s_new, d_fox)], w_out_c_b, o, vec(norm_mix_post[layer]), xs)

        g_next = vec(norm_mix_pre[layer + 1]) if layer + 1 < depth else vec(norm_mix_pre[layer])
        xp, xpn = mlp_residual(xp, vec(norm_mlp_pre[layer]), w_up_b, w_down_b, layer, vec(norm_mlp_post[layer]), g_next)
        xs, xsn = mlp_residual(xs, vec(norm_mlp_pre[layer]), w_up_b, w_down_b, layer, vec(norm_mlp_post[layer]), g_next)

    y_prompt = xp[:, n_meta:]
    y_sample = xs.reshape(bs, s_new, d_model)
    return (y_prompt, y_sample,
            jnp.stack(conv_p), jnp.stack(c_p), jnp.stack(n_p), jnp.stack(m_p),
            jnp.stack(k_p), jnp.stack(v_p), jnp.stack(lf_p),
            jnp.stack(conv_s), jnp.stack(c_s), jnp.stack(n_s), jnp.stack(m_s),
            jnp.stack(k_s), jnp.stack(v_s), jnp.stack(lf_s))
```
